```python
import jax, jax.numpy as jnp
from jax import lax
import numpy as np

D_MODEL = 1024
BATCH = 8
SEQ = 2048
DEPTH = 4

CHUNK = 64
MIX_W = D_MODEL // 2
SB_HEAD_DIM = 64
SB_HEADS = MIX_W // SB_HEAD_DIM
SB_BLOCK = 128
RWKV_HEAD_DIM = 64
RWKV_HEADS = MIX_W // RWKV_HEAD_DIM
DECAY_LORA = 64
ICL_LORA = 64
GATE_LORA = 128
RWKV_IN = 3 * MIX_W + DECAY_LORA + ICL_LORA + GATE_LORA
HGRN_EXPAND = 128
HGRN_HEADS = MIX_W // HGRN_EXPAND
N_BRANCH = 3
IN_COLS = 3 * MIX_W + RWKV_IN + 4 * MIX_W + N_BRANCH * D_MODEL
D_FF = -(-(8 * D_MODEL) // (3 * 256)) * 256
ALPHA = (2 * DEPTH) ** 0.25
BETA = (8 * DEPTH) ** -0.25
LN_EPS = 1e-5
RWKV_GN_EPS = 64e-5
RMS_EPS = 1e-6

kernel_name = 'hybrid_sb_rwkv7_hgrn2_deepnorm'

F32 = jnp.float32


def _split(p, sizes):
    idx = [int(i) for i in np.cumsum(sizes)[:-1]]
    return jnp.split(p, idx, axis=-1)


def _heads(z, n):
    return z.reshape(*z.shape[:-1], n, z.shape[-1] // n)


def _layer_norm(x, g, b):
    xf = x.astype(F32)
    mu = jnp.mean(xf, -1, keepdims=True)
    var = jnp.mean(jnp.square(xf - mu), -1, keepdims=True)
    return ((xf - mu) * lax.rsqrt(var + LN_EPS)).astype(x.dtype) * g + b


def _token_shift(p):
    return jnp.pad(p[:, :-1], ((0, 0), (1, 0), (0, 0)))


def stick_breaking_attention(q, k, v):
    B, T, H, Dh = q.shape
    qh = jnp.swapaxes(q, 1, 2).astype(F32) * (Dh ** -0.5)
    kh = jnp.swapaxes(k, 1, 2).astype(F32)
    vh = jnp.swapaxes(v, 1, 2).astype(F32)
    outs = []
    for start in range(0, T, SB_BLOCK):
        end = start + SB_BLOCK
        z = jnp.einsum('bhqd,bhkd->bhqk', qh[:, :, start:end], kh[:, :, :end])
        mask = jnp.arange(end)[None, :] < jnp.arange(start, end)[:, None]
        log_keep = jnp.where(mask, jax.nn.log_sigmoid(-z), 0.0)
        log_after = lax.cumsum(log_keep, axis=3, reverse=True) - log_keep
        w = jnp.where(mask, jnp.exp(jax.nn.log_sigmoid(z) + log_after), 0.0)
        outs.append(jnp.einsum('bhqk,bhkd->bhqd', w, vh[:, :, :end]))
    o = jnp.concatenate(outs, axis=2)
    return jnp.swapaxes(o, 1, 2).reshape(B, T, H * Dh)


def _rwkv7_scan(r, decay, k, v, kk, a):
    B, T, H, N = r.shape

    def step(S, inp):
        r_t, w_t, k_t, v_t, kk_t, a_t = inp
        sa = jnp.einsum('bhvk,bhk->bhv', S, -kk_t)
        S = (S * w_t[:, :, None, :] + sa[..., None] * (kk_t * a_t)[:, :, None, :]
             + v_t[..., None] * k_t[:, :, None, :])
        return S, jnp.einsum('bhvk,bhk->bhv', S, r_t)

    xs = tuple(jnp.moveaxis(z, 1, 0) for z in (r, decay, k, v, kk, a))
    _, ys = lax.scan(step, jnp.zeros((B, H, N, N), F32), xs)
    return jnp.moveaxis(ys, 0, 1)


def rwkv7_time_mix(p, mu, w0, w_up, a0, a_up, g_up, k_k, k_a, r_k, gn_g, gn_b):
    B, T, _ = p.shape
    p = p + (_token_shift(p) - p) * mu
    r, k, v, wd, ad, gd = _split(p, [MIX_W] * 3 + [DECAY_LORA, ICL_LORA, GATE_LORA])
    w = -jax.nn.softplus(-(w0 + jnp.tanh(wd) @ w_up)) - 0.5
    decay = jnp.exp(-jnp.exp(w.astype(F32)))
    a = jax.nn.sigmoid(a0 + ad @ a_up)
    g = jax.nn.sigmoid(gd) @ g_up
    hs = lambda z: _heads(z, RWKV_HEADS).astype(F32)
    kk = hs(k * k_k)
    kk = kk * lax.rsqrt(jnp.maximum(jnp.sum(kk * kk, -1, keepdims=True), 1e-24))
    k = k * (1 + (a - 1) * k_a)
    rh, kh, vh = hs(r), hs(k), hs(v)
    y = _rwkv7_scan(rh, hs(decay), kh, vh, kk, hs(a))
    mu_y = jnp.mean(y, -1, keepdims=True)
    var_y = jnp.mean(jnp.square(y - mu_y), -1, keepdims=True)
    y = ((y - mu_y) * lax.rsqrt(var_y + RWKV_GN_EPS)).reshape(B, T, MIX_W) * gn_g + gn_b
    bonus = jnp.sum(rh * kh * _heads(r_k, RWKV_HEADS).astype(F32), -1, keepdims=True) * vh
    return (y + bonus.reshape(B, T, MIX_W)) * g


def _hgrn2_chunked(q, k, v, log_f):
    B, T, H, Dk = q.shape
    Dv = v.shape[-1]
    NC = T // CHUNK
    to_c = lambda z: z.reshape(B, NC, CHUNK, H, z.shape[-1]).transpose(1, 0, 3, 2, 4)
    tri = jnp.tril(jnp.ones((CHUNK, CHUNK), bool))

    def chunk_step(S, inp):
        qc, kc, vc, lfc = inp
        b = jnp.cumsum(lfc, axis=2)
        rel = b[:, :, :, None, :] - b[:, :, None, :, :]
        dec = jnp.exp(jnp.where(tri[:, :, None], rel, -jnp.inf))
        scores = jnp.einsum('bhtc,bhsc,bhtsc->bhts', qc, kc, dec)
        o = (jnp.einsum('bhts,bhsv->bhtv', scores, vc)
             + jnp.einsum('bhtc,bhcv->bhtv', qc * jnp.exp(b), S))
        b_last = b[:, :, -1:, :]
        S = (jnp.exp(b_last[:, :, 0, :])[..., None] * S
             + jnp.einsum('bhsc,bhsv->bhcv', kc * jnp.exp(b_last - b), vc))
        return S, o

    xs = tuple(to_c(z.astype(F32)) for z in (q, k, v, log_f))
    _, outs = lax.scan(chunk_step, jnp.zeros((B, H, Dk, Dv), F32), xs)
    return outs.transpose(1, 0, 3, 2, 4).reshape(B, T, H, Dv)


def hgrn2_mix(q, f, i, g, lb, norm_g):
    B, T, _ = q.shape
    fgate = lb + (1 - lb) * jax.nn.sigmoid(f.astype(F32))
    hs = lambda z: _heads(z, HGRN_HEADS)
    o = _hgrn2_chunked(hs(q), hs(1 - fgate), hs(jax.nn.silu(i)), hs(jnp.log(fgate)))
    o = o * lax.rsqrt(jnp.mean(jnp.square(o), -1, keepdims=True) + RMS_EPS)
    return o.reshape(B, T, MIX_W) * norm_g * jax.nn.silu(g)


def _swiglu(h, w_in, w_out):
    a, b = jnp.split(h @ w_in, 2, axis=-1)
    return (jax.nn.silu(a) * b) @ w_out


def setup_inputs(seed: int = 0) -> dict:
    key = jax.random.key(seed)
    ks = jax.random.split(key, 26)
    n = lambda k, s, sc: jax.random.normal(k, s, F32) * sc
    return {
        'x': n(ks[0], (BATCH, SEQ, D_MODEL), 1.0),
        'ln_in_g': 1.0 + n(ks[1], (D_MODEL,), 0.02),
        'ln_in_b': n(ks[2], (D_MODEL,), 0.02),
        'w_in': n(ks[3], (DEPTH, D_MODEL, IN_COLS), D_MODEL ** -0.5),
        'rwkv_mu': jax.random.uniform(ks[4], (DEPTH, RWKV_IN), F32),
        'rwkv_w0': n(ks[5], (DEPTH, MIX_W), 0.5),
        'rwkv_w_up': n(ks[6], (DEPTH, DECAY_LORA, MIX_W), DECAY_LORA ** -0.5),
        'rwkv_a0': n(ks[7], (DEPTH, MIX_W), 0.1),
        'rwkv_a_up': n(ks[8], (DEPTH, ICL_LORA, MIX_W), ICL_LORA ** -0.5),
        'rwkv_g_up': n(ks[9], (DEPTH, GATE_LORA, MIX_W), GATE_LORA ** -0.5),
        'rwkv_k_k': 0.85 + n(ks[10], (DEPTH, MIX_W), 0.05),
        'rwkv_k_a': 1.0 + n(ks[11], (DEPTH, MIX_W), 0.05),
        'rwkv_r_k': n(ks[12], (DEPTH, MIX_W), 0.1),
        'rwkv_ln_g': 1.0 + n(ks[13], (DEPTH, MIX_W), 0.02),
        'rwkv_ln_b': n(ks[14], (DEPTH, MIX_W), 0.02),
        'hgrn_lb_logits': 1.0 + n(ks[15], (DEPTH, MIX_W), 0.1),
        'hgrn_norm_g': 1.0 + n(ks[16], (DEPTH, MIX_W), 0.02),
        'w_branch_up': n(ks[17], (DEPTH, N_BRANCH, MIX_W, D_MODEL), MIX_W ** -0.5),
        'w_out': n(ks[18], (DEPTH, D_MODEL, D_MODEL), BETA * D_MODEL ** -0.5),
        'ln1_g': 1.0 + n(ks[19], (DEPTH, D_MODEL), 0.02),
        'ln1_b': n(ks[20], (DEPTH, D_MODEL), 0.02),
        'w_ffn_in': n(ks[21], (DEPTH, D_MODEL, 2 * D_FF), D_MODEL ** -0.5),
        'w_ffn_out': n(ks[22], (DEPTH, D_FF, D_MODEL), BETA * D_FF ** -0.5),
        'ln2_g': 1.0 + n(ks[23], (DEPTH, D_MODEL), 0.02),
        'ln2_b': n(ks[24], (DEPTH, D_MODEL), 0.02),
    }


def reference(x, ln_in_g, ln_in_b, w_in, rwkv_mu, rwkv_w0, rwkv_w_up, rwkv_a0, rwkv_a_up,
              rwkv_g_up, rwkv_k_k, rwkv_k_a, rwkv_r_k, rwkv_ln_g, rwkv_ln_b, hgrn_lb_logits,
              hgrn_norm_g, w_branch_up, w_out, ln1_g, ln1_b, w_ffn_in, w_ffn_out, ln2_g, ln2_b):
    B, T, _ = x.shape
    h = _layer_norm(x, ln_in_g, ln_in_b)
    lb_p = jax.nn.softmax(hgrn_lb_logits.astype(F32), axis=0)
    lower_bounds = jnp.cumsum(lb_p, axis=0) - lb_p[0:1]
    for l in range(DEPTH):
        p = h @ w_in[l]
        q_sb, k_sb, v_sb, p_rw, q_hg, f_hg, i_hg, g_hg, p_gate = _split(
            p, [MIX_W] * 3 + [RWKV_IN] + [MIX_W] * 4 + [N_BRANCH * D_MODEL])
        y_sb = stick_breaking_attention(_heads(q_sb, SB_HEADS), _heads(k_sb, SB_HEADS),
                                        _heads(v_sb, SB_HEADS))
        y_rw = rwkv7_time_mix(p_rw, rwkv_mu[l], rwkv_w0[l], rwkv_w_up[l], rwkv_a0[l],
                              rwkv_a_up[l], rwkv_g_up[l], rwkv_k_k[l], rwkv_k_a[l],
                              rwkv_r_k[l], rwkv_ln_g[l], rwkv_ln_b[l])
        y_hg = hgrn2_mix(q_hg, f_hg, i_hg, g_hg, lower_bounds[l], hgrn_norm_g[l])
        gates = jax.nn.sigmoid(p_gate).reshape(B, T, N_BRANCH, D_MODEL)
        merged = (gates[:, :, 0] * (y_sb @ w_branch_up[l, 0])
                  + gates[:, :, 1] * (y_rw @ w_branch_up[l, 1])
                  + gates[:, :, 2] * (y_hg @ w_branch_up[l, 2]))
        h = _layer_norm(ALPHA * h + merged @ w_out[l], ln1_g[l], ln1_b[l])
        h = _layer_norm(ALPHA * h + _swiglu(h, w_ffn_in[l], w_ffn_out[l]), ln2_g[l], ln2_b[l])
    return h.astype(x.dtype)
```

```python
import functools

import jax
import jax.numpy as jnp
from jax import lax
from jax.experimental import pallas as pl
from jax.experimental.pallas import tpu as pltpu

F32 = jnp.float32
BF16 = jnp.bfloat16

D_MODEL = 1024
DEPTH = 4
MIX_W = D_MODEL // 2
HEAD_DIM = 64
DECAY_LORA = 64
ICL_LORA = 64
GATE_LORA = 128
RWKV_IN = 3 * MIX_W + DECAY_LORA + ICL_LORA + GATE_LORA
HGRN_EXPAND = 128
HGRN_HEADS = MIX_W // HGRN_EXPAND
N_BRANCH = 3
D_FF = -(-(8 * D_MODEL) // (3 * 256)) * 256
ALPHA = (2 * DEPTH) ** 0.25
LN_EPS = 1e-5
RWKV_GN_EPS = 64e-5
RMS_EPS = 1e-6

LANES = 128
SB_BLK = 128
SB_DEAD = 104.0
RW_CHUNK = 64
HG_CHUNK = 16
HG_TILE = 256
VMEM_LIMIT = 56 * 1024 * 1024

_NT = (((1,), (1,)), ((), ()))
_TN = (((0,), (0,)), ((), ()))


def _dot(a, b):
    return jnp.dot(a, b, preferred_element_type=F32)


def _dot_nt(a, b):
    return lax.dot_general(a, b, _NT, preferred_element_type=F32)


def _dot_tn(a, b):
    return lax.dot_general(a, b, _TN, preferred_element_type=F32)


def _dot2(x, w):
    hi = x.astype(BF16)
    lo = (x - hi.astype(F32)).astype(BF16)
    return _dot(hi, w) + _dot(lo, w)


def _sigmoid(x):
    return 1.0 / (1.0 + jnp.exp(-x))


def _softplus(x):
    return jnp.maximum(x, 0.0) + jnp.log(1.0 + jnp.exp(-jnp.abs(x)))


def _layer_norm(x, g, b):
    mu = jnp.mean(x, -1, keepdims=True)
    xc = x - mu
    var = jnp.mean(xc * xc, -1, keepdims=True)
    return xc * lax.rsqrt(var + LN_EPS) * g + b


def _params(*sem):
    return pltpu.CompilerParams(dimension_semantics=sem, vmem_limit_bytes=VMEM_LIMIT)


def _row_tile(m, want):
    t = min(m, want)
    assert m % t == 0
    return t


def _ln_kernel(x_ref, g_ref, b_ref, o_ref):
    o_ref[...] = _layer_norm(x_ref[...], g_ref[...], b_ref[...])


def _entry_ln(x2, g, b):
    m, d = x2.shape
    tm = _row_tile(m, 512)
    return pl.pallas_call(
        _ln_kernel,
        grid=(m // tm,),
        in_specs=[pl.BlockSpec((tm, d), lambda i: (i, 0)),
                  pl.BlockSpec((1, d), lambda i: (0, 0)),
                  pl.BlockSpec((1, d), lambda i: (0, 0))],
        out_specs=pl.BlockSpec((tm, d), lambda i: (i, 0)),
        out_shape=jax.ShapeDtypeStruct((m, d), F32),
        compiler_params=_params("parallel"),
        name="entry_ln",
    )(x2, g.reshape(1, d), b.reshape(1, d))


def _proj_kernel(h_ref, w_ref, o_ref, *, gate):
    hb = h_ref[...].astype(BF16)
    n = w_ref.shape[1]
    step = 512 if n % 512 == 0 else 256
    for n0 in range(0, n, step):
        y = _dot(hb, w_ref[:, n0:n0 + step])
        if gate:
            y = _sigmoid(y)
        o_ref[:, n0:n0 + step] = y.astype(o_ref.dtype)


def _project(h, w, out_dtype, gate=False):
    m, d = h.shape
    n = w.shape[1]
    tm = _row_tile(m, 512)
    return pl.pallas_call(
        functools.partial(_proj_kernel, gate=gate),
        grid=(m // tm,),
        in_specs=[pl.BlockSpec((tm, d), lambda i: (i, 0)),
                  pl.BlockSpec((d, n), lambda i: (0, 0))],
        out_specs=pl.BlockSpec((tm, n), lambda i: (i, 0)),
        out_shape=jax.ShapeDtypeStruct((m, n), out_dtype),
        compiler_params=_params("parallel"),
        name="gate_proj" if gate else "in_proj",
    )(h, w)


def _sb_kernel(q_ref, k_ref, v_ref, o_ref):
    blk = SB_BLK
    qi = pl.program_id(2)
    lane = lax.broadcasted_iota(jnp.int32, (1, LANES), 1)
    head_masks = (lane < HEAD_DIM, lane >= HEAD_DIM)
    row = lax.broadcasted_iota(jnp.int32, (blk, blk), 0)
    col = lax.broadcasted_iota(jnp.int32, (blk, blk), 1)
    later = (row > col).astype(BF16)
    strict = col < row
    q = (q_ref[...].astype(F32) * (HEAD_DIM ** -0.5)).astype(BF16)
    zero_q = jnp.zeros_like(q)

    def cond(carry):
        kb, rmin = carry[0], carry[1]
        return jnp.logical_and(kb >= 0, rmin < SB_DEAD)

    def body(carry):
        kb, _, run0, run1, acc = carry
        start = pl.multiple_of(kb * blk, blk)
        kblk = k_ref[pl.ds(start, blk), :]
        vblk = v_ref[pl.ds(start, blk), :]
        valid = jnp.logical_or(kb < qi, strict)
        runs = []
        for mask, run in zip(head_masks, (run0, run1)):
            z = _dot_nt(jnp.where(mask, q, zero_q), kblk)
            sp = _softplus(z)
            spm = jnp.where(valid, sp, 0.0)
            after = _dot2(spm, later) + run
            w = jnp.where(valid, jnp.exp((z - sp) - after), 0.0)
            acc = acc + _dot(w.astype(BF16), jnp.where(mask, vblk, jnp.zeros_like(vblk)))
            runs.append(run + jnp.sum(spm, axis=1, keepdims=True))
        rmin = jnp.min(jnp.minimum(runs[0], runs[1]))
        return kb - 1, rmin, runs[0], runs[1], acc

    init = (qi, jnp.float32(0.0), jnp.zeros((blk, 1), F32), jnp.zeros((blk, 1), F32),
            jnp.zeros((blk, LANES), F32))
    acc = lax.while_loop(cond, body, init)[4]
    o_ref[...] = acc.astype(o_ref.dtype)


def _sb_attention(qkv, batch, seq):
    blk = SB_BLK
    ncol = MIX_W // LANES
    x3 = qkv.reshape(batch, seq, 3 * MIX_W)
    out = pl.pallas_call(
        _sb_kernel,
        grid=(batch, ncol, seq // blk),
        in_specs=[pl.BlockSpec((None, blk, LANES), lambda b, c, i: (b, i, c)),
                  pl.BlockSpec((None, seq, LANES), lambda b, c, i: (b, 0, ncol + c)),
                  pl.BlockSpec((None, seq, LANES), lambda b, c, i: (b, 0, 2 * ncol + c))],
        out_specs=pl.BlockSpec((None, blk, LANES), lambda b, c, i: (b, i, c)),
        out_shape=jax.ShapeDtypeStruct((batch, seq, MIX_W), BF16),
        compiler_params=_params("parallel", "parallel", "arbitrary"),
        name="stick_breaking",
    )(x3, x3, x3)
    return out.reshape(batch * seq, MIX_W)


def _head_sum_matrix(width, head):
    r = lax.broadcasted_iota(jnp.int32, (width, width), 0) // head
    c = lax.broadcasted_iota(jnp.int32, (width, width), 1) // head
    return (r == c).astype(BF16)


def _rwkv_prep_kernel(p_ref, prev_ref, mu_ref, w0_ref, wup_ref, a0_ref, aup_ref, gup_ref,
                      kk_ref, ka_ref, rk_ref,
                      r_out, lw_out, k_out, v_out, kkn_out, b_out, g_out, bonus_out, *, tiles_per_seq):
    tm = p_ref.shape[0]
    p = p_ref[...]
    first = pl.program_id(0) % tiles_per_seq == 0
    prev = jnp.where(first, 0.0, prev_ref[7:8, :])
    rowid = lax.broadcasted_iota(jnp.int32, (tm, 1), 0)
    shifted = jnp.where(rowid == 0, prev, pltpu.roll(p, 1, 0))
    xs = p + (shifted - p) * mu_ref[...]
    w = MIX_W
    r = xs[:, 0:w]
    k = xs[:, w:2 * w]
    v = xs[:, 2 * w:3 * w]
    lora_in = xs[:, 3 * w:3 * w + LANES]
    gd = xs[:, 3 * w + LANES:3 * w + 2 * LANES]
    dec = w0_ref[...] + _dot(jnp.tanh(lora_in).astype(BF16), wup_ref[...])
    lw_out[...] = -jnp.exp(-_softplus(-dec) - 0.5)
    a = _sigmoid(a0_ref[...] + _dot(lora_in.astype(BF16), aup_ref[...]))
    g_out[...] = _dot(_sigmoid(gd).astype(BF16), gup_ref[...])
    hsum = _head_sum_matrix(w, HEAD_DIM)
    kk = k * kk_ref[...]
    kkn = kk * lax.rsqrt(jnp.maximum(_dot2(kk * kk, hsum), 1e-24))
    k2 = k * (1.0 + (a - 1.0) * ka_ref[...])
    r_out[...] = r
    k_out[...] = k2
    v_out[...] = v
    kkn_out[...] = kkn
    b_out[...] = kkn * a
    bonus_out[...] = _dot2(r * k2 * rk_ref[...], hsum) * v


def _rwkv_prep(p_rw, seq, mu, w0, w_up_pad, a0, a_up_pad, g_up, k_k, k_a, r_k):
    m, n = p_rw.shape
    tm = _row_tile(seq, 256)
    w = MIX_W
    vec = lambda width: pl.BlockSpec((1, width), lambda i: (0, 0))
    mat = lambda rows: pl.BlockSpec((rows, w), lambda i: (0, 0))
    out = pl.BlockSpec((tm, w), lambda i: (i, 0))
    return pl.pallas_call(
        functools.partial(_rwkv_prep_kernel, tiles_per_seq=seq // tm),
        grid=(m // tm,),
        in_specs=[pl.BlockSpec((tm, n), lambda i: (i, 0)),
                  pl.BlockSpec((8, n), lambda i: (jnp.maximum(i * (tm // 8) - 1, 0), 0)),
                  vec(n), vec(w), mat(LANES), vec(w), mat(LANES), mat(LANES),
                  vec(w), vec(w), vec(w)],
        out_specs=[out] * 8,
        out_shape=[jax.ShapeDtypeStruct((m, w), F32)] * 8,
        compiler_params=_params("parallel"),
        name="rwkv_prep",
    )(p_rw, p_rw, mu.reshape(1, n), w0.reshape(1, w), w_up_pad, a0.reshape(1, w), a_up_pad, g_up,
      k_k.reshape(1, w), k_a.reshape(1, w), r_k.reshape(1, w))


def _rwkv_scan_kernel(r_ref, lw_ref, k_ref, v_ref, kk_ref, b_ref, g_ref, bonus_ref, gng_ref, gnb_ref,
                      o_ref, state_ref):
    c = RW_CHUNK
    c2 = 2 * c
    seq = r_ref.shape[0]
    lane = lax.broadcasted_iota(jnp.int32, (1, LANES), 1)
    m0 = (lane < HEAD_DIM).astype(F32)
    m1 = 1.0 - m0
    row = lax.broadcasted_iota(jnp.int32, (c2, c2), 0)
    col = lax.broadcasted_iota(jnp.int32, (c2, c2), 1)
    same_head = (row // c) == (col // c)
    strict = jnp.logical_and(same_head, (row % c) > (col % c))
    incl = jnp.logical_and(same_head, (row % c) >= (col % c))
    eye = (row == col).astype(F32)
    head_block = same_head.astype(F32)
    trow = lax.broadcasted_iota(jnp.int32, (c, c), 0)
    tcol = lax.broadcasted_iota(jnp.int32, (c, c), 1)
    cum = (trow >= tcol).astype(BF16)
    hsum = _head_sum_matrix(LANES, HEAD_DIM)
    gng = gng_ref[...]
    gnb = gnb_ref[...]

    def stack(x):
        return jnp.concatenate([x * m0, x * m1], axis=0)

    state_ref[...] = jnp.zeros_like(state_ref)

    def chunk(ci, _):
        rows = pl.ds(pl.multiple_of(ci * c, c), c)
        r, lw, k, v, kk, bv = (x[rows, :] for x in (r_ref, lw_ref, k_ref, v_ref, kk_ref, b_ref))
        lw_hi = lw.astype(BF16)
        lw_lo = (lw - lw_hi.astype(F32)).astype(BF16)
        cs = _dot(cum, lw_hi) + _dot(cum, lw_lo)
        tot = cs[c - 1:c, :]
        p_inc = jnp.exp(cs)
        p_exc = jnp.exp(cs - lw)
        p_inv = jnp.exp(-cs)
        p_end = jnp.exp(tot - cs)
        left = jnp.concatenate([stack(kk * p_exc), stack(r * p_inc)], axis=0).astype(BF16)
        right = jnp.concatenate([stack(k * p_inv), stack(bv * p_inv)], axis=0).astype(BF16)
        scores = _dot_nt(left, right)
        a_kk = jnp.where(strict, scores[0:c2, 0:c2], 0.0)
        a_kb = jnp.where(strict, scores[0:c2, c2:2 * c2], 0.0)
        a_rk = jnp.where(incl, scores[c2:2 * c2, 0:c2], 0.0)
        a_rb = jnp.where(incl, scores[c2:2 * c2, c2:2 * c2], 0.0)
        ab = a_kb.astype(BF16)
        inv = eye - a_kb
        power = _dot(ab, ab)
        for lvl in range(5):
            pb = power.astype(BF16)
            inv = inv + _dot(inv.astype(BF16), pb)
            if lvl < 4:
                power = _dot(pb, pb)
        state = state_ref[...]
        from_state = _dot_nt(left, state.astype(BF16))
        v_stack = stack(v).astype(BF16)
        from_v = _dot(jnp.concatenate([a_kk, a_rk], axis=0).astype(BF16), v_stack)
        u_stack = _dot(inv.astype(BF16), (from_state[0:c2] + from_v[0:c2]).astype(BF16))
        y_stack = from_state[c2:] + from_v[c2:] - _dot(a_rb.astype(BF16), u_stack.astype(BF16))
        y = y_stack[0:c] + y_stack[c:]
        u = u_stack[0:c] + u_stack[c:]
        vu = jnp.concatenate([v, u], axis=0).astype(BF16)
        kb_end = jnp.concatenate([k * p_end, -(bv * p_end)], axis=0).astype(BF16)
        state_ref[...] = state * jnp.exp(tot) + head_block * _dot_tn(vu, kb_end)
        mean = _dot2(y, hsum) * (1.0 / HEAD_DIM)
        yc = y - mean
        var = _dot2(yc * yc, hsum) * (1.0 / HEAD_DIM)
        yn = yc * lax.rsqrt(var + RWKV_GN_EPS) * gng + gnb
        o_ref[rows, :] = ((yn + bonus_ref[rows, :]) * g_ref[rows, :]).astype(o_ref.dtype)
        return 0

    lax.fori_loop(0, seq // c, chunk, 0)


def _rwkv_scan(prep, batch, seq, gn_g, gn_b):
    ncol = MIX_W // LANES
    blk = pl.BlockSpec((None, seq, LANES), lambda b, c: (b, 0, c))
    vec = pl.BlockSpec((1, LANES), lambda b, c: (0, c))
    args = [x.reshape(batch, seq, MIX_W) for x in prep]
    out = pl.pallas_call(
        _rwkv_scan_kernel,
        grid=(batch, ncol),
        in_specs=[blk] * 8 + [vec, vec],
        out_specs=blk,
        out_shape=jax.ShapeDtypeStruct((batch, seq, MIX_W), BF16),
        scratch_shapes=[pltpu.VMEM((LANES, LANES), F32)],
        compiler_params=_params("parallel", "parallel"),
        name="rwkv_scan",
    )(*args, gn_g.reshape(1, MIX_W), gn_b.reshape(1, MIX_W))
    return out.reshape(batch * seq, MIX_W)


def _hgrn_kernel(q_ref, f_ref, i_ref, g_ref, lbl_ref, ng_ref, o_ref, state_ref, *, layer):
    hc = HG_CHUNK
    tile = min(HG_TILE, q_ref.shape[0])
    nch = tile // hc
    seq = q_ref.shape[0]
    logits = lbl_ref[...]
    e = jnp.exp(logits - jnp.max(logits, axis=0, keepdims=True))
    probs = e / jnp.sum(e, axis=0, keepdims=True)
    lb = jnp.zeros((1, LANES), F32)
    for j in range(1, layer + 1):
        lb = lb + probs[j:j + 1, :]
    ng = ng_ref[...]
    pos = lax.broadcasted_iota(jnp.int32, (tile, 1), 0) % hc
    pos3 = lax.broadcasted_iota(jnp.int32, (nch, hc, 1), 1)
    state_ref[...] = jnp.zeros_like(state_ref)

    def tile_body(ti, _):
        rows = pl.ds(pl.multiple_of(ti * tile, tile), tile)
        q = q_ref[rows, :]
        fgate = lb + (1.0 - lb) * _sigmoid(f_ref[rows, :])
        xi = i_ref[rows, :]
        v = xi * _sigmoid(xi)
        k = 1.0 - fgate
        b = jnp.log(fgate)
        for d in (1, 2, 4, 8):
            b = b + jnp.where(pos >= d, pltpu.roll(b, d, 0), 0.0)
        b3 = b.reshape(nch, hc, LANES)
        q3 = q.reshape(nch, hc, LANES)
        k3 = k.reshape(nch, hc, LANES)
        v3 = v.reshape(nch, hc, LANES)
        o3 = jnp.zeros((nch, hc, LANES), F32)
        for s in range(hc):
            dec = jnp.exp(jnp.where(pos3 >= s, b3 - b3[:, s:s + 1, :], -1e30))
            wgt = jnp.sum(q3 * k3[:, s:s + 1, :] * dec, axis=-1, keepdims=True)
            o3 = o3 + wgt * v3[:, s:s + 1, :]
        b_last = b3[:, hc - 1:hc, :]
        qd = (q3 * jnp.exp(b3)).astype(BF16)
        k_end = (k3 * jnp.exp(b_last - b3)).astype(BF16)
        carry = jnp.exp(b_last)
        vb = v3.astype(BF16)
        state = state_ref[...]
        inter = []
        for ch in range(nch):
            inter.append(_dot_nt(qd[ch], state.astype(BF16)))
            state = state * carry[ch] + _dot_tn(vb[ch], k_end[ch])
        state_ref[...] = state
        o = o3.reshape(tile, LANES) + jnp.concatenate(inter, axis=0)
        o = o * lax.rsqrt(jnp.mean(o * o, axis=-1, keepdims=True) + RMS_EPS)
        gt = g_ref[rows, :]
        o_ref[rows, :] = (o * ng * (gt * _sigmoid(gt))).astype(o_ref.dtype)
        return 0

    lax.fori_loop(0, seq // tile, tile_body, 0)


def _hgrn(p_hg, batch, seq, lb_logits, norm_g, layer):
    nh = HGRN_HEADS
    x3 = p_hg.reshape(batch, seq, 4 * MIX_W)
    col = lambda j: pl.BlockSpec((None, seq, LANES), lambda b, h: (b, 0, j * nh + h))
    out = pl.pallas_call(
        functools.partial(_hgrn_kernel, layer=layer),
        grid=(batch, nh),
        in_specs=[col(0), col(1), col(2), col(3),
                  pl.BlockSpec((DEPTH, LANES), lambda b, h: (0, h)),
                  pl.BlockSpec((1, LANES), lambda b, h: (0, h))],
        out_specs=pl.BlockSpec((None, seq, LANES), lambda b, h: (b, 0, h)),
        out_shape=jax.ShapeDtypeStruct((batch, seq, MIX_W), BF16),
        scratch_shapes=[pltpu.VMEM((LANES, LANES), F32)],
        compiler_params=_params("parallel", "parallel"),
        name="hgrn2",
    )(x3, x3, x3, x3, lb_logits, norm_g.reshape(1, MIX_W))
    return out.reshape(batch * seq, MIX_W)


def _merge_kernel(h_ref, ysb_ref, yrw_ref, yhg_ref, gate_ref, wup_ref, wout_ref, g_ref, b_ref, o_ref):
    d = D_MODEL
    merged = None
    for j, y_ref in enumerate((ysb_ref, yrw_ref, yhg_ref)):
        term = gate_ref[:, j * d:(j + 1) * d] * _dot(y_ref[...], wup_ref[j])
        merged = term if merged is None else merged + term
    x = ALPHA * h_ref[...] + _dot(merged.astype(BF16), wout_ref[...])
    o_ref[...] = _layer_norm(x, g_ref[...], b_ref[...])


def _merge(h, y_sb, y_rw, y_hg, gates, w_up, w_out, g, b):
    m, d = h.shape
    tm = _row_tile(m, 256)
    rows = lambda width: pl.BlockSpec((tm, width), lambda i: (i, 0))
    return pl.pallas_call(
        _merge_kernel,
        grid=(m // tm,),
        in_specs=[rows(d), rows(MIX_W), rows(MIX_W), rows(MIX_W), rows(N_BRANCH * d),
                  pl.BlockSpec((N_BRANCH, MIX_W, d), lambda i: (0, 0, 0)),
                  pl.BlockSpec((d, d), lambda i: (0, 0)),
                  pl.BlockSpec((1, d), lambda i: (0, 0)),
                  pl.BlockSpec((1, d), lambda i: (0, 0))],
        out_specs=rows(d),
        out_shape=jax.ShapeDtypeStruct((m, d), F32),
        compiler_params=_params("parallel"),
        name="merge_out",
    )(h, y_sb, y_rw, y_hg, gates, w_up, w_out, g.reshape(1, d), b.reshape(1, d))


def _ffn_kernel(h_ref, wa_ref, wb_ref, wo_ref, g_ref, b_ref, o_ref, acc_ref):
    j = pl.program_id(1)

    @pl.when(j == 0)
    def _():
        acc_ref[...] = jnp.zeros_like(acc_ref)

    hb = h_ref[...].astype(BF16)
    a = _dot(hb, wa_ref[...])
    act = (a * _sigmoid(a) * _dot(hb, wb_ref[...])).astype(BF16)
    acc_ref[...] += _dot(act, wo_ref[...])

    @pl.when(j == pl.num_programs(1) - 1)
    def _():
        o_ref[...] = _layer_norm(ALPHA * h_ref[...] + acc_ref[...], g_ref[...], b_ref[...])


def _ffn(h, w_in, w_out, g, b):
    m, d = h.shape
    tm = _row_tile(m, 512)
    tf = D_FF // 2
    nf = D_FF // tf
    return pl.pallas_call(
        _ffn_kernel,
        grid=(m // tm, nf),
        in_specs=[pl.BlockSpec((tm, d), lambda i, j: (i, 0)),
                  pl.BlockSpec((d, tf), lambda i, j: (0, j)),
                  pl.BlockSpec((d, tf), lambda i, j: (0, nf + j)),
                  pl.BlockSpec((tf, d), lambda i, j: (j, 0)),
                  pl.BlockSpec((1, d), lambda i, j: (0, 0)),
                  pl.BlockSpec((1, d), lambda i, j: (0, 0))],
        out_specs=pl.BlockSpec((tm, d), lambda i, j: (i, 0)),
        out_shape=jax.ShapeDtypeStruct((m, d), F32),
        scratch_shapes=[pltpu.VMEM((tm, d), F32)],
        compiler_params=_params("parallel", "arbitrary"),
        name="ffn",
    )(h, w_in, w_in, w_out, g.reshape(1, d), b.reshape(1, d))


def kernel(x, ln_in_g, ln_in_b, w_in, rwkv_mu, rwkv_w0, rwkv_w_up, rwkv_a0, rwkv_a_up, rwkv_g_up,
           rwkv_k_k, rwkv_k_a, rwkv_r_k, rwkv_ln_g, rwkv_ln_b, hgrn_lb_logits, hgrn_norm_g,
           w_branch_up, w_out, ln1_g, ln1_b, w_ffn_in, w_ffn_out, ln2_g, ln2_b):
    batch, seq, d = x.shape
    m = batch * seq
    w = MIX_W
    offs = (0, 3 * w, 3 * w + RWKV_IN, 3 * w + RWKV_IN + 4 * w, w_in.shape[-1])
    w_in_b = w_in.astype(BF16)
    zeros_lora = jnp.zeros((DEPTH, DECAY_LORA, w), BF16)
    w_up_pad = jnp.concatenate([rwkv_w_up.astype(BF16), zeros_lora], axis=1)
    a_up_pad = jnp.concatenate([zeros_lora, rwkv_a_up.astype(BF16)], axis=1)
    g_up_b = rwkv_g_up.astype(BF16)
    w_branch_b = w_branch_up.astype(BF16)
    w_out_b = w_out.astype(BF16)
    w_ffn_in_b = w_ffn_in.astype(BF16)
    w_ffn_out_b = w_ffn_out.astype(BF16)
    lb_logits = hgrn_lb_logits.astype(F32)

    h = _entry_ln(x.reshape(m, d), ln_in_g, ln_in_b)
    for l in range(DEPTH):
        qkv = _project(h, w_in_b[l, :, offs[0]:offs[1]], BF16)
        p_rw = _project(h, w_in_b[l, :, offs[1]:offs[2]], F32)
        p_hg = _project(h, w_in_b[l, :, offs[2]:offs[3]], F32)
        gates = _project(h, w_in_b[l, :, offs[3]:offs[4]], F32, gate=True)
        y_sb = _sb_attention(qkv, batch, seq)
        prep = _rwkv_prep(p_rw, seq, rwkv_mu[l], rwkv_w0[l], w_up_pad[l], rwkv_a0[l], a_up_pad[l],
                          g_up_b[l], rwkv_k_k[l], rwkv_k_a[l], rwkv_r_k[l])
        y_rw = _rwkv_scan(prep, batch, seq, rwkv_ln_g[l], rwkv_ln_b[l])
        y_hg = _hgrn(p_hg, batch, seq, lb_logits, hgrn_norm_g[l], l)
        h = _merge(h, y_sb, y_rw, y_hg, gates, w_branch_b[l], w_out_b[l], ln1_g[l], ln1_b[l])
        h = _ffn(h, w_ffn_in_b[l], w_ffn_out_b[l], ln2_g[l], ln2_b[l])
    return h.reshape(batch, seq, d).astype(x.dtype)
```

```python
import functools

import jax
import jax.numpy as jnp
from jax import lax
from jax.experimental import pallas as pl
from jax.experimental.pallas import tpu as pltpu

F32 = jnp.float32
BF16 = jnp.bfloat16

D_MODEL = 1024
DEPTH = 4
MIX_W = D_MODEL // 2
HEAD_DIM = 64
DECAY_LORA = 64
ICL_LORA = 64
GATE_LORA = 128
RWKV_IN = 3 * MIX_W + DECAY_LORA + ICL_LORA + GATE_LORA
HGRN_EXPAND = 128
HGRN_HEADS = MIX_W // HGRN_EXPAND
N_BRANCH = 3
D_FF = -(-(8 * D_MODEL) // (3 * 256)) * 256
ALPHA = (2 * DEPTH) ** 0.25
LN_EPS = 1e-5
RWKV_GN_EPS = 64e-5
RMS_EPS = 1e-6

LANES = 128
SB_BLK = 128
SB_TILE = 512
SB_DEAD = 104.0
RW_CHUNK = 64
RW_TILE = 256
HG_CHUNK = 16
HG_TILE = 256
VMEM_LIMIT = 56 * 1024 * 1024

_NT = (((1,), (1,)), ((), ()))
_TN = (((0,), (0,)), ((), ()))


def _dot(a, b):
    return jnp.dot(a, b, preferred_element_type=F32)


def _dot_nt(a, b):
    return lax.dot_general(a, b, _NT, preferred_element_type=F32)


def _dot_tn(a, b):
    return lax.dot_general(a, b, _TN, preferred_element_type=F32)


def _dot2(x, w):
    hi = x.astype(BF16)
    lo = (x - hi.astype(F32)).astype(BF16)
    return _dot(hi, w) + _dot(lo, w)


def _sigmoid(x):
    return 1.0 / (1.0 + jnp.exp(-x))


def _softplus(x):
    return jnp.maximum(x, 0.0) + jnp.log(1.0 + jnp.exp(-jnp.abs(x)))


def _layer_norm(x, g, b):
    mu = jnp.mean(x, -1, keepdims=True)
    xc = x - mu
    var = jnp.mean(xc * xc, -1, keepdims=True)
    return xc * lax.rsqrt(var + LN_EPS) * g + b


def _params(*sem):
    return pltpu.CompilerParams(dimension_semantics=sem, vmem_limit_bytes=VMEM_LIMIT)


def _row_tile(m, want):
    t = min(m, want)
    assert m % t == 0
    return t


def _ln_kernel(x_ref, g_ref, b_ref, o_ref):
    o_ref[...] = _layer_norm(x_ref[...], g_ref[...], b_ref[...])


def _entry_ln(x2, g, b):
    m, d = x2.shape
    tm = _row_tile(m, 512)
    return pl.pallas_call(
        _ln_kernel,
        grid=(m // tm,),
        in_specs=[pl.BlockSpec((tm, d), lambda i: (i, 0)),
                  pl.BlockSpec((1, d), lambda i: (0, 0)),
                  pl.BlockSpec((1, d), lambda i: (0, 0))],
        out_specs=pl.BlockSpec((tm, d), lambda i: (i, 0)),
        out_shape=jax.ShapeDtypeStruct((m, d), F32),
        compiler_params=_params("parallel"),
        name="entry_ln",
    )(x2, g.reshape(1, d), b.reshape(1, d))


def _proj_kernel(h_ref, w_ref, o_ref, *, gate):
    hb = h_ref[...].astype(BF16)
    n = w_ref.shape[1]
    step = 512 if n % 512 == 0 else 256
    for n0 in range(0, n, step):
        y = _dot(hb, w_ref[:, n0:n0 + step])
        if gate:
            y = _sigmoid(y)
        o_ref[:, n0:n0 + step] = y.astype(o_ref.dtype)


def _project(h, w, out_dtype, gate=False):
    m, d = h.shape
    n = w.shape[1]
    tm = _row_tile(m, 512)
    return pl.pallas_call(
        functools.partial(_proj_kernel, gate=gate),
        grid=(m // tm,),
        in_specs=[pl.BlockSpec((tm, d), lambda i: (i, 0)),
                  pl.BlockSpec((d, n), lambda i: (0, 0))],
        out_specs=pl.BlockSpec((tm, n), lambda i: (i, 0)),
        out_shape=jax.ShapeDtypeStruct((m, n), out_dtype),
        compiler_params=_params("parallel"),
        name="gate_proj" if gate else "in_proj",
    )(h, w)


def _sb_kernel(q_ref, k_ref, v_ref, o_ref):
    blk = SB_BLK
    nq = q_ref.shape[0] // blk
    qb0 = pl.program_id(2) * nq
    lane = lax.broadcasted_iota(jnp.int32, (1, LANES), 1)
    m0 = lane < HEAD_DIM
    row = lax.broadcasted_iota(jnp.int32, (2 * blk, blk), 0) % blk
    col = lax.broadcasted_iota(jnp.int32, (2 * blk, blk), 1)
    strict = col < row
    krow = lax.broadcasted_iota(jnp.int32, (blk, blk), 0)
    kcol = lax.broadcasted_iota(jnp.int32, (blk, blk), 1)
    later = (krow > kcol).astype(BF16)
    q = (q_ref[...].astype(F32) * (HEAD_DIM ** -0.5)).astype(BF16)
    zq = jnp.zeros((blk, LANES), BF16)
    qs = [jnp.concatenate([jnp.where(m0, q[j * blk:(j + 1) * blk], zq),
                           jnp.where(m0, zq, q[j * blk:(j + 1) * blk])], axis=0) for j in range(nq)]

    def step(kbs, runs, accs, diagonal):
        js = range(nq)
        starts = [pl.multiple_of(jnp.maximum(kb, 0) * blk, blk) for kb in kbs]
        z = [_dot_nt(qs[j], k_ref[pl.ds(starts[j], blk), :]) for j in js]
        sp = [_softplus(z[j]) for j in js]
        spm = [jnp.where(strict, sp[j], 0.0) for j in js] if diagonal else sp
        within = _dot(jnp.concatenate([x.astype(BF16) for x in spm], axis=0), later)
        w = [jnp.exp((z[j] - sp[j]) - (within[j * 2 * blk:(j + 1) * 2 * blk] + runs[j])) for j in js]
        if diagonal:
            w = [jnp.where(strict, x, 0.0) for x in w]
        new_accs = []
        for j in js:
            vblk = v_ref[pl.ds(starts[j], blk), :]
            wcat = jnp.concatenate([w[j][0:blk], w[j][blk:]], axis=1).astype(BF16)
            vcat = jnp.concatenate([jnp.where(m0, vblk, zq), jnp.where(m0, zq, vblk)], axis=0)
            new_accs.append(accs[j] + _dot(wcat, vcat))
        return [runs[j] + jnp.sum(spm[j], axis=1, keepdims=True) for j in js], new_accs

    runs, accs = step([qb0 + j for j in range(nq)], [jnp.zeros((2 * blk, 1), F32)] * nq,
                      [jnp.zeros((blk, LANES), F32)] * nq, True)

    def alive_after(d, runs):
        alive = jnp.bool_(False)
        for j in range(nq):
            alive = jnp.logical_or(alive, jnp.logical_and(qb0 + j - d - 1 >= 0, jnp.min(runs[j]) < SB_DEAD))
        return alive

    def cond(carry):
        return carry[1]

    def body(carry):
        d, _, runs, accs = carry
        kbs = [qb0 + j - d for j in range(nq)]
        runs = [jnp.where(kbs[j] >= 0, runs[j], 1e30) for j in range(nq)]
        new_runs, new_accs = step(kbs, runs, accs, False)
        return d + 1, alive_after(d, new_runs), new_runs, new_accs

    accs = lax.while_loop(cond, body, (jnp.int32(1), alive_after(0, runs), runs, accs))[3]
    for j in range(nq):
        o_ref[j * blk:(j + 1) * blk, :] = accs[j].astype(o_ref.dtype)


def _sb_attention(qkv, batch, seq):
    blk = _row_tile(seq, SB_TILE)
    ncol = MIX_W // LANES
    x3 = qkv.reshape(batch, seq, 3 * MIX_W)
    out = pl.pallas_call(
        _sb_kernel,
        grid=(batch, ncol, seq // blk),
        in_specs=[pl.BlockSpec((None, blk, LANES), lambda b, c, i: (b, i, c)),
                  pl.BlockSpec((None, seq, LANES), lambda b, c, i: (b, 0, ncol + c)),
                  pl.BlockSpec((None, seq, LANES), lambda b, c, i: (b, 0, 2 * ncol + c))],
        out_specs=pl.BlockSpec((None, blk, LANES), lambda b, c, i: (b, i, c)),
        out_shape=jax.ShapeDtypeStruct((batch, seq, MIX_W), BF16),
        compiler_params=_params("parallel", "parallel", "arbitrary"),
        name="stick_breaking",
    )(x3, x3, x3)
    return out.reshape(batch * seq, MIX_W)


def _head_sum_matrix(width, head):
    r = lax.broadcasted_iota(jnp.int32, (width, width), 0) // head
    c = lax.broadcasted_iota(jnp.int32, (width, width), 1) // head
    return (r == c).astype(BF16)


def _rwkv_kernel(p_ref, prev_ref, mu_ref, w0_ref, wup_ref, a0_ref, aup_ref, gup_ref,
                 kk_ref, ka_ref, rk_ref, gng_ref, gnb_ref, o_ref,
                 state_ref, kkd_s, rd_s, kinv_s, binv_s, kend_s, bend_s, v_s, y_s, g_s, bonus_s):
    tt = p_ref.shape[0]
    c = RW_CHUNK
    c2 = 2 * c
    nch = tt // c
    w = MIX_W
    npair = w // LANES
    ti = pl.program_id(1)

    @pl.when(ti == 0)
    def _():
        state_ref[...] = jnp.zeros_like(state_ref)

    p = p_ref[...]
    prev = jnp.where(ti == 0, 0.0, prev_ref[7:8, :])
    rowid = lax.broadcasted_iota(jnp.int32, (tt, 1), 0)
    shifted = jnp.where(rowid == 0, prev, pltpu.roll(p, 1, 0))
    xs = p + (shifted - p) * mu_ref[...]
    r = xs[:, 0:w]
    k = xs[:, w:2 * w]
    v = xs[:, 2 * w:3 * w]
    lora_in = xs[:, 3 * w:3 * w + LANES]
    gd = xs[:, 3 * w + LANES:3 * w + 2 * LANES]
    dec = w0_ref[...] + _dot(jnp.tanh(lora_in).astype(BF16), wup_ref[...])
    lw = -jnp.exp(-_softplus(-dec) - 0.5)
    a = _sigmoid(a0_ref[...] + _dot(lora_in.astype(BF16), aup_ref[...]))
    g_s[...] = _dot(_sigmoid(gd).astype(BF16), gup_ref[...])
    hsum = _head_sum_matrix(LANES, HEAD_DIM)

    def head_sum(x):
        return jnp.concatenate([_dot(x[:, i * LANES:(i + 1) * LANES].astype(BF16), hsum)
                                for i in range(npair)], axis=1)

    kk = k * kk_ref[...]
    kkn = kk * lax.rsqrt(jnp.maximum(head_sum(kk * kk), 1e-24))
    k2 = k * (1.0 + (a - 1.0) * ka_ref[...])
    bvec = kkn * a
    bonus_s[...] = head_sum(r * k2 * rk_ref[...]) * v
    trow = lax.broadcasted_iota(jnp.int32, (tt, tt), 0)
    tcol = lax.broadcasted_iota(jnp.int32, (tt, tt), 1)
    cum = jnp.logical_and(trow // c == tcol // c, trow >= tcol).astype(BF16)
    lw_hi = lw.astype(BF16)
    lw_lo = (lw - lw_hi.astype(F32)).astype(BF16)
    cs = _dot(cum, lw_hi) + _dot(cum, lw_lo)
    tot = cs.reshape(nch, c, w)[:, c - 1:c, :]
    to_end = (tot - cs.reshape(nch, c, w)).reshape(tt, w)
    p_inv = jnp.exp(-cs)
    p_end = jnp.exp(to_end)
    kkd_s[...] = (kkn * jnp.exp(cs - lw)).astype(BF16)
    rd_s[...] = (r * jnp.exp(cs)).astype(BF16)
    kinv_s[...] = (k2 * p_inv).astype(BF16)
    binv_s[...] = (bvec * p_inv).astype(BF16)
    kend_s[...] = (k2 * p_end).astype(BF16)
    bend_s[...] = (-(bvec * p_end)).astype(BF16)
    v_s[...] = v.astype(BF16)
    chunk_decay = jnp.exp(tot)

    lane = lax.broadcasted_iota(jnp.int32, (1, LANES), 1)
    m0 = lane < HEAD_DIM
    row = lax.broadcasted_iota(jnp.int32, (c2, c2), 0)
    col = lax.broadcasted_iota(jnp.int32, (c2, c2), 1)
    same_head = (row // c) == (col // c)
    strict = jnp.logical_and(same_head, (row % c) > (col % c))
    incl = jnp.logical_and(same_head, (row % c) >= (col % c))
    eye = (row == col).astype(F32)
    head_block = same_head.astype(F32)
    zb = jnp.zeros((c, LANES), BF16)

    def stack(x):
        return jnp.concatenate([jnp.where(m0, x, zb), jnp.where(m0, zb, x)], axis=0)

    units = [(ci, pi) for ci in range(nch) for pi in range(npair)]
    rows = {u: slice(u[0] * c, (u[0] + 1) * c) for u in units}
    lanes = {u: slice(u[1] * LANES, (u[1] + 1) * LANES) for u in units}
    left = {u: jnp.concatenate([stack(kkd_s[rows[u], lanes[u]]), stack(rd_s[rows[u], lanes[u]])], axis=0)
            for u in units}
    scores = {u: _dot_nt(left[u], jnp.concatenate([stack(kinv_s[rows[u], lanes[u]]),
                                                   stack(binv_s[rows[u], lanes[u]])], axis=0))
              for u in units}
    a_kb = {u: jnp.where(strict, scores[u][0:c2, c2:2 * c2], 0.0) for u in units}
    a_rb = {u: jnp.where(incl, scores[u][c2:2 * c2, c2:2 * c2], 0.0).astype(BF16) for u in units}
    from_v = {u: _dot(jnp.concatenate([jnp.where(strict, scores[u][0:c2, 0:c2], 0.0),
                                       jnp.where(incl, scores[u][c2:2 * c2, 0:c2], 0.0)], axis=0).astype(BF16),
                      stack(v_s[rows[u], lanes[u]])) for u in units}
    inv = {u: eye - a_kb[u] for u in units}
    power = {}
    for u in units:
        ab = a_kb[u].astype(BF16)
        power[u] = _dot(ab, ab).astype(BF16)
    for lvl in range(5):
        for u in units:
            if lvl < 4:
                both = _dot(jnp.concatenate([inv[u].astype(BF16), power[u]], axis=0), power[u])
                inv[u] = inv[u] + both[0:c2]
                power[u] = both[c2:].astype(BF16)
            else:
                inv[u] = (inv[u] + _dot(inv[u].astype(BF16), power[u])).astype(BF16)
    states = [state_ref[pi] for pi in range(npair)]
    for ci in range(nch):
        cu = [(ci, pi) for pi in range(npair)]
        from_state = {u: _dot_nt(left[u], states[u[1]].astype(BF16)) for u in cu}
        u_stack = {u: _dot(inv[u], (from_state[u][0:c2] + from_v[u][0:c2]).astype(BF16)) for u in cu}
        y_stack = {u: from_state[u][c2:] + from_v[u][c2:] - _dot(a_rb[u], u_stack[u].astype(BF16)) for u in cu}
        for u in cu:
            uu = u_stack[u][0:c] + u_stack[u][c:]
            vu = jnp.concatenate([v_s[rows[u], lanes[u]], uu.astype(BF16)], axis=0)
            kb_end = jnp.concatenate([kend_s[rows[u], lanes[u]], bend_s[rows[u], lanes[u]]], axis=0)
            states[u[1]] = (states[u[1]] * chunk_decay[ci, :, lanes[u]]
                            + head_block * _dot_tn(vu, kb_end))
            y_s[rows[u], lanes[u]] = y_stack[u][0:c] + y_stack[u][c:]
    for pi in range(npair):
        state_ref[pi] = states[pi]

    for pi in range(npair):
        ls = slice(pi * LANES, (pi + 1) * LANES)
        y = y_s[:, ls]
        mean = _dot(y.astype(BF16), hsum) * (1.0 / HEAD_DIM)
        yc = y - mean
        var = _dot((yc * yc).astype(BF16), hsum) * (1.0 / HEAD_DIM)
        yn = yc * lax.rsqrt(var + RWKV_GN_EPS) * gng_ref[:, ls] + gnb_ref[:, ls]
        o_ref[:, ls] = ((yn + bonus_s[:, ls]) * g_s[:, ls]).astype(o_ref.dtype)


def _rwkv(p_rw, batch, seq, mu, w0, w_up_pad, a0, a_up_pad, g_up, k_k, k_a, r_k, gn_g, gn_b):
    n = p_rw.shape[1]
    w = MIX_W
    tt = _row_tile(seq, RW_TILE)
    nt = seq // tt
    vec = lambda width: pl.BlockSpec((1, width), lambda b, t: (0, 0))
    mat = pl.BlockSpec((LANES, w), lambda b, t: (0, 0))
    tile_bf16 = pltpu.VMEM((tt, w), BF16)
    tile_f32 = pltpu.VMEM((tt, w), F32)
    return pl.pallas_call(
        _rwkv_kernel,
        grid=(batch, nt),
        in_specs=[pl.BlockSpec((tt, n), lambda b, t: (b * nt + t, 0)),
                  pl.BlockSpec((8, n), lambda b, t: (jnp.maximum((b * nt + t) * (tt // 8) - 1, 0), 0)),
                  vec(n), vec(w), mat, vec(w), mat, mat, vec(w), vec(w), vec(w), vec(w), vec(w)],
        out_specs=pl.BlockSpec((tt, w), lambda b, t: (b * nt + t, 0)),
        out_shape=jax.ShapeDtypeStruct((batch * seq, w), BF16),
        scratch_shapes=[pltpu.VMEM((w // LANES, LANES, LANES), F32)] + [tile_bf16] * 7 + [tile_f32] * 3,
        compiler_params=_params("parallel", "arbitrary"),
        name="rwkv7",
    )(p_rw, p_rw, mu.reshape(1, n), w0.reshape(1, w), w_up_pad, a0.reshape(1, w), a_up_pad, g_up,
      k_k.reshape(1, w), k_a.reshape(1, w), r_k.reshape(1, w), gn_g.reshape(1, w), gn_b.reshape(1, w))


def _hgrn_kernel(q_ref, f_ref, i_ref, g_ref, lbl_ref, ng_ref, o_ref, state_ref, *, layer):
    hc = HG_CHUNK
    tile = min(HG_TILE, q_ref.shape[0])
    nch = tile // hc
    seq = q_ref.shape[0]
    logits = lbl_ref[...]
    e = jnp.exp(logits - jnp.max(logits, axis=0, keepdims=True))
    probs = e / jnp.sum(e, axis=0, keepdims=True)
    lb = jnp.zeros((1, LANES), F32)
    for j in range(1, layer + 1):
        lb = lb + probs[j:j + 1, :]
    ng = ng_ref[...]
    pos = lax.broadcasted_iota(jnp.int32, (tile, 1), 0) % hc
    pos3 = lax.broadcasted_iota(jnp.int32, (nch, hc, 1), 1)
    state_ref[...] = jnp.zeros_like(state_ref)

    def tile_body(ti, _):
        rows = pl.ds(pl.multiple_of(ti * tile, tile), tile)
        q = q_ref[rows, :]
        fgate = lb + (1.0 - lb) * _sigmoid(f_ref[rows, :])
        xi = i_ref[rows, :]
        v = xi * _sigmoid(xi)
        k = 1.0 - fgate
        b = jnp.log(fgate)
        for d in (1, 2, 4, 8):
            b = b + jnp.where(pos >= d, pltpu.roll(b, d, 0), 0.0)
        b3 = b.reshape(nch, hc, LANES)
        q3 = q.reshape(nch, hc, LANES)
        k3 = k.reshape(nch, hc, LANES)
        v3 = v.reshape(nch, hc, LANES)
        o3 = jnp.zeros((nch, hc, LANES), F32)
        for s in range(hc):
            dec = jnp.exp(jnp.where(pos3 >= s, b3 - b3[:, s:s + 1, :], -1e30))
            wgt = jnp.sum(q3 * k3[:, s:s + 1, :] * dec, axis=-1, keepdims=True)
            o3 = o3 + wgt * v3[:, s:s + 1, :]
        b_last = b3[:, hc - 1:hc, :]
        qd = (q3 * jnp.exp(b3)).astype(BF16)
        k_end = (k3 * jnp.exp(b_last - b3)).astype(BF16)
        carry = jnp.exp(b_last)
        vb = v3.astype(BF16)
        updates = [_dot_tn(vb[ch], k_end[ch]) for ch in range(nch)]
        states = [state_ref[...]]
        for ch in range(nch):
            states.append(states[ch] * carry[ch] + updates[ch])
        state_ref[...] = states[nch]
        inter = [_dot_nt(qd[ch], states[ch].astype(BF16)) for ch in range(nch)]
        o = o3.reshape(tile, LANES) + jnp.concatenate(inter, axis=0)
        o = o * lax.rsqrt(jnp.mean(o * o, axis=-1, keepdims=True) + RMS_EPS)
        gt = g_ref[rows, :]
        o_ref[rows, :] = (o * ng * (gt * _sigmoid(gt))).astype(o_ref.dtype)
        return 0

    lax.fori_loop(0, seq // tile, tile_body, 0)


def _hgrn(p_hg, batch, seq, lb_logits, norm_g, layer):
    nh = HGRN_HEADS
    x3 = p_hg.reshape(batch, seq, 4 * MIX_W)
    col = lambda j: pl.BlockSpec((None, seq, LANES), lambda b, h: (b, 0, j * nh + h))
    out = pl.pallas_call(
        functools.partial(_hgrn_kernel, layer=layer),
        grid=(batch, nh),
        in_specs=[col(0), col(1), col(2), col(3),
                  pl.BlockSpec((DEPTH, LANES), lambda b, h: (0, h)),
                  pl.BlockSpec((1, LANES), lambda b, h: (0, h))],
        out_specs=pl.BlockSpec((None, seq, LANES), lambda b, h: (b, 0, h)),
        out_shape=jax.ShapeDtypeStruct((batch, seq, MIX_W), BF16),
        scratch_shapes=[pltpu.VMEM((LANES, LANES), F32)],
        compiler_params=_params("parallel", "parallel"),
        name="hgrn2",
    )(x3, x3, x3, x3, lb_logits, norm_g.reshape(1, MIX_W))
    return out.reshape(batch * seq, MIX_W)


def _merge_kernel(h_ref, ysb_ref, yrw_ref, yhg_ref, gate_ref, wup_ref, wout_ref, g_ref, b_ref, o_ref):
    d = D_MODEL
    merged = None
    for j, y_ref in enumerate((ysb_ref, yrw_ref, yhg_ref)):
        term = gate_ref[:, j * d:(j + 1) * d] * _dot(y_ref[...], wup_ref[j])
        merged = term if merged is None else merged + term
    x = ALPHA * h_ref[...] + _dot(merged.astype(BF16), wout_ref[...])
    o_ref[...] = _layer_norm(x, g_ref[...], b_ref[...])


def _merge(h, y_sb, y_rw, y_hg, gates, w_up, w_out, g, b):
    m, d = h.shape
    tm = _row_tile(m, 256)
    rows = lambda width: pl.BlockSpec((tm, width), lambda i: (i, 0))
    return pl.pallas_call(
        _merge_kernel,
        grid=(m // tm,),
        in_specs=[rows(d), rows(MIX_W), rows(MIX_W), rows(MIX_W), rows(N_BRANCH * d),
                  pl.BlockSpec((N_BRANCH, MIX_W, d), lambda i: (0, 0, 0)),
                  pl.BlockSpec((d, d), lambda i: (0, 0)),
                  pl.BlockSpec((1, d), lambda i: (0, 0)),
                  pl.BlockSpec((1, d), lambda i: (0, 0))],
        out_specs=rows(d),
        out_shape=jax.ShapeDtypeStruct((m, d), F32),
        compiler_params=_params("parallel"),
        name="merge_out",
    )(h, y_sb, y_rw, y_hg, gates, w_up, w_out, g.reshape(1, d), b.reshape(1, d))


def _ffn_kernel(h_ref, wa_ref, wb_ref, wo_ref, g_ref, b_ref, o_ref, acc_ref):
    j = pl.program_id(1)

    @pl.when(j == 0)
    def _():
        acc_ref[...] = jnp.zeros_like(acc_ref)

    hb = h_ref[...].astype(BF16)
    a = _dot(hb, wa_ref[...])
    act = (a * _sigmoid(a) * _dot(hb, wb_ref[...])).astype(BF16)
    acc_ref[...] += _dot(act, wo_ref[...])

    @pl.when(j == pl.num_programs(1) - 1)
    def _():
        o_ref[...] = _layer_norm(ALPHA * h_ref[...] + acc_ref[...], g_ref[...], b_ref[...])


def _ffn(h, w_in, w_out, g, b):
    m, d = h.shape
    tm = _row_tile(m, 512)
    tf = D_FF // 2
    nf = D_FF // tf
    return pl.pallas_call(
        _ffn_kernel,
        grid=(m // tm, nf),
        in_specs=[pl.BlockSpec((tm, d), lambda i, j: (i, 0)),
                  pl.BlockSpec((d, tf), lambda i, j: (0, j)),
                  pl.BlockSpec((d, tf), lambda i, j: (0, nf + j)),
                  pl.BlockSpec((tf, d), lambda i, j: (j, 0)),
                  pl.BlockSpec((1, d), lambda i, j: (0, 0)),
                  pl.BlockSpec((1, d), lambda i, j: (0, 0))],
        out_specs=pl.BlockSpec((tm, d), lambda i, j: (i, 0)),
        out_shape=jax.ShapeDtypeStruct((m, d), F32),
        scratch_shapes=[pltpu.VMEM((tm, d), F32)],
        compiler_params=_params("parallel", "arbitrary"),
        name="ffn",
    )(h, w_in, w_in, w_out, g.reshape(1, d), b.reshape(1, d))


def kernel(x, ln_in_g, ln_in_b, w_in, rwkv_mu, rwkv_w0, rwkv_w_up, rwkv_a0, rwkv_a_up, rwkv_g_up,
           rwkv_k_k, rwkv_k_a, rwkv_r_k, rwkv_ln_g, rwkv_ln_b, hgrn_lb_logits, hgrn_norm_g,
           w_branch_up, w_out, ln1_g, ln1_b, w_ffn_in, w_ffn_out, ln2_g, ln2_b):
    batch, seq, d = x.shape
    m = batch * seq
    w = MIX_W
    offs = (0, 3 * w, 3 * w + RWKV_IN, 3 * w + RWKV_IN + 4 * w, w_in.shape[-1])
    w_in_b = w_in.astype(BF16)
    zeros_lora = jnp.zeros((DEPTH, DECAY_LORA, w), BF16)
    w_up_pad = jnp.concatenate([rwkv_w_up.astype(BF16), zeros_lora], axis=1)
    a_up_pad = jnp.concatenate([zeros_lora, rwkv_a_up.astype(BF16)], axis=1)
    g_up_b = rwkv_g_up.astype(BF16)
    w_branch_b = w_branch_up.astype(BF16)
    w_out_b = w_out.astype(BF16)
    w_ffn_in_b = w_ffn_in.astype(BF16)
    w_ffn_out_b = w_ffn_out.astype(BF16)
    lb_logits = hgrn_lb_logits.astype(F32)

    h = _entry_ln(x.reshape(m, d), ln_in_g, ln_in_b)
    for l in range(DEPTH):
        qkv = _project(h, w_in_b[l, :, offs[0]:offs[1]], BF16)
        p_rw = _project(h, w_in_b[l, :, offs[1]:offs[2]], F32)
        p_hg = _project(h, w_in_b[l, :, offs[2]:offs[3]], F32)
        gates = _project(h, w_in_b[l, :, offs[3]:offs[4]], F32, gate=True)
        y_sb = _sb_attention(qkv, batch, seq)
        y_rw = _rwkv(p_rw, batch, seq, rwkv_mu[l], rwkv_w0[l], w_up_pad[l], rwkv_a0[l], a_up_pad[l],
                     g_up_b[l], rwkv_k_k[l], rwkv_k_a[l], rwkv_r_k[l], rwkv_ln_g[l], rwkv_ln_b[l])
        y_hg = _hgrn(p_hg, batch, seq, lb_logits, hgrn_norm_g[l], l)
        h = _merge(h, y_sb, y_rw, y_hg, gates, w_branch_b[l], w_out_b[l], ln1_g[l], ln1_b[l])
        h = _ffn(h, w_ffn_in_b[l], w_ffn_out_b[l], ln2_g[l], ln2_b[l])
    return h.reshape(batch, seq, d).astype(x.dtype)
```

```python
import functools

import jax
import jax.numpy as jnp
from jax import lax
from jax.experimental import pallas as pl
from jax.experimental.pallas import tpu as pltpu

F32 = jnp.float32
BF16 = jnp.bfloat16

D_MODEL = 1024
DEPTH = 4
MIX_W = D_MODEL // 2
HEAD_DIM = 64
DECAY_LORA = 64
ICL_LORA = 64
GATE_LORA = 128
RWKV_IN = 3 * MIX_W + DECAY_LORA + ICL_LORA + GATE_LORA
HGRN_EXPAND = 128
HGRN_HEADS = MIX_W // HGRN_EXPAND
N_BRANCH = 3
D_FF = -(-(8 * D_MODEL) // (3 * 256)) * 256
ALPHA = (2 * DEPTH) ** 0.25
LN_EPS = 1e-5
RWKV_GN_EPS = 64e-5
RMS_EPS = 1e-6

LANES = 128
SUBLANES = 8
LOG2E = 1.4426950408889634
SB_BLK = 128
SB_TILE = 512
SB_SUBSTEPS = 2
SB_DEAD = 104.0
RW_CHUNK = 64
RW_TILE = 256
HG_CHUNK = 16
HG_TILE = 256
DENSE_SUBTILE = 128
VMEM_LIMIT = 56 * 1024 * 1024

_NT = (((1,), (1,)), ((), ()))
_TN = (((0,), (0,)), ((), ()))


def _dot(a, b):
    return jnp.dot(a, b, preferred_element_type=F32)


def _dot_nt(a, b):
    return lax.dot_general(a, b, _NT, preferred_element_type=F32)


def _dot_tn(a, b):
    return lax.dot_general(a, b, _TN, preferred_element_type=F32)


def _dot2(x, w):
    hi = x.astype(BF16)
    lo = (x - hi.astype(F32)).astype(BF16)
    return _dot(hi, w) + _dot(lo, w)


def _sigmoid(x):
    return 1.0 / (1.0 + jnp.exp(-x))


def _softplus(x):
    return jnp.maximum(x, 0.0) + jnp.log(1.0 + jnp.exp(-jnp.abs(x)))


def _layer_norm(x, g, b):
    mu = jnp.mean(x, -1, keepdims=True)
    xc = x - mu
    var = jnp.mean(xc * xc, -1, keepdims=True)
    return xc * lax.rsqrt(var + LN_EPS) * g + b


def _params(*sem):
    return pltpu.CompilerParams(dimension_semantics=sem, vmem_limit_bytes=VMEM_LIMIT)


def _row_tile(m, want):
    t = min(m, want)
    assert m % t == 0
    return t


def _ln_kernel(x_ref, g_ref, b_ref, o_ref):
    o_ref[...] = _layer_norm(x_ref[...], g_ref[...], b_ref[...])


def _entry_ln(x2, g, b):
    m, d = x2.shape
    tm = _row_tile(m, 512)
    return pl.pallas_call(
        _ln_kernel,
        grid=(m // tm,),
        in_specs=[pl.BlockSpec((tm, d), lambda i: (i, 0)),
                  pl.BlockSpec((1, d), lambda i: (0, 0)),
                  pl.BlockSpec((1, d), lambda i: (0, 0))],
        out_specs=pl.BlockSpec((tm, d), lambda i: (i, 0)),
        out_shape=jax.ShapeDtypeStruct((m, d), F32),
        compiler_params=_params("parallel"),
        name="entry_ln",
    )(x2, g.reshape(1, d), b.reshape(1, d))


def _proj_kernel(h_ref, w_ref, o_ref, *, gate):
    hb = h_ref[...].astype(BF16)
    n = w_ref.shape[1]
    step = 512 if n % 512 == 0 else 256
    for n0 in range(0, n, step):
        y = _dot(hb, w_ref[:, n0:n0 + step])
        if gate:
            y = _sigmoid(y)
        o_ref[:, n0:n0 + step] = y.astype(o_ref.dtype)


def _project(h, w, out_dtype, gate=False):
    m, d = h.shape
    n = w.shape[1]
    tm = _row_tile(m, 512)
    return pl.pallas_call(
        functools.partial(_proj_kernel, gate=gate),
        grid=(m // tm,),
        in_specs=[pl.BlockSpec((tm, d), lambda i: (i, 0)),
                  pl.BlockSpec((d, n), lambda i: (0, 0))],
        out_specs=pl.BlockSpec((tm, n), lambda i: (i, 0)),
        out_shape=jax.ShapeDtypeStruct((m, n), out_dtype),
        compiler_params=_params("parallel"),
        name="gate_proj" if gate else "in_proj",
    )(h, w)


def _sb_kernel(q_ref, k_ref, v_ref, o_ref):
    blk = SB_BLK
    nq = q_ref.shape[0] // blk
    qb0 = pl.program_id(2) * nq
    lane = lax.broadcasted_iota(jnp.int32, (1, LANES), 1)
    m0 = lane < HEAD_DIM
    row = lax.broadcasted_iota(jnp.int32, (2 * blk, blk), 0) % blk
    col = lax.broadcasted_iota(jnp.int32, (2 * blk, blk), 1)
    strict = col < row
    krow = lax.broadcasted_iota(jnp.int32, (blk, blk), 0)
    kcol = lax.broadcasted_iota(jnp.int32, (blk, blk), 1)
    later = (krow > kcol).astype(BF16)
    q = (q_ref[...].astype(F32) * (HEAD_DIM ** -0.5)).astype(BF16)
    zq = jnp.zeros((blk, LANES), BF16)
    qs = [jnp.concatenate([jnp.where(m0, q[j * blk:(j + 1) * blk], zq),
                           jnp.where(m0, zq, q[j * blk:(j + 1) * blk])], axis=0) for j in range(nq)]

    def steps(d, nsub, runs, accs, diagonal):
        units = [(s, j) for s in range(nsub) for j in range(nq)]
        kb = {u: qb0 + u[1] - d - u[0] for u in units}
        start = {u: pl.multiple_of(jnp.maximum(kb[u], 0) * blk, blk) for u in units}
        z = {u: _dot_nt(qs[u[1]], k_ref[pl.ds(start[u], blk), :]) for u in units}
        sp, spm, within = {}, {}, {}
        for u in units:
            sp[u] = _softplus(z[u])
            spm[u] = jnp.where(strict, sp[u], 0.0) if diagonal else sp[u]
            within[u] = _dot(spm[u].astype(BF16), later)
        new_runs, new_accs = [], []
        for j in range(nq):
            run = runs[j]
            ws, vs = [], []
            for s in range(nsub):
                u = (s, j)
                if not diagonal:
                    run = jnp.where(kb[u] >= 0, run, 1e30)
                w = jnp.exp((z[u] - sp[u]) - (within[u] + run))
                if diagonal:
                    w = jnp.where(strict, w, 0.0)
                vblk = v_ref[pl.ds(start[u], blk), :]
                ws += [w[0:blk], w[blk:]]
                vs += [jnp.where(m0, vblk, zq), jnp.where(m0, zq, vblk)]
                run = run + jnp.sum(spm[u], axis=1, keepdims=True)
            new_accs.append(accs[j] + _dot(jnp.concatenate(ws, axis=1).astype(BF16),
                                           jnp.concatenate(vs, axis=0)))
            new_runs.append(run)
        return new_runs, new_accs

    runs, accs = steps(0, 1, [jnp.zeros((2 * blk, 1), F32)] * nq, [jnp.zeros((blk, LANES), F32)] * nq, True)

    def alive_after(d, runs):
        alive = jnp.bool_(False)
        for j in range(nq):
            alive = jnp.logical_or(alive, jnp.logical_and(qb0 + j - d - 1 >= 0, jnp.min(runs[j]) < SB_DEAD))
        return alive

    def cond(carry):
        return carry[1]

    def body(carry):
        d, _, runs, accs = carry
        new_runs, new_accs = steps(d, SB_SUBSTEPS, runs, accs, False)
        return d + SB_SUBSTEPS, alive_after(d + SB_SUBSTEPS - 1, new_runs), new_runs, new_accs

    accs = lax.while_loop(cond, body, (jnp.int32(1), alive_after(0, runs), runs, accs))[3]
    for j in range(nq):
        o_ref[j * blk:(j + 1) * blk, :] = accs[j].astype(o_ref.dtype)


def _sb_attention(qkv, batch, seq):
    blk = _row_tile(seq, SB_TILE)
    ncol = MIX_W // LANES
    x3 = qkv.reshape(batch, seq, 3 * MIX_W)
    out = pl.pallas_call(
        _sb_kernel,
        grid=(batch, ncol, seq // blk),
        in_specs=[pl.BlockSpec((None, blk, LANES), lambda b, c, i: (b, i, c)),
                  pl.BlockSpec((None, seq, LANES), lambda b, c, i: (b, 0, ncol + c)),
                  pl.BlockSpec((None, seq, LANES), lambda b, c, i: (b, 0, 2 * ncol + c))],
        out_specs=pl.BlockSpec((None, blk, LANES), lambda b, c, i: (b, i, c)),
        out_shape=jax.ShapeDtypeStruct((batch, seq, MIX_W), BF16),
        compiler_params=_params("parallel", "parallel", "arbitrary"),
        name="stick_breaking",
    )(x3, x3, x3)
    return out.reshape(batch * seq, MIX_W)


def _head_sum_matrix(width, head):
    r = lax.broadcasted_iota(jnp.int32, (width, width), 0) // head
    c = lax.broadcasted_iota(jnp.int32, (width, width), 1) // head
    return (r == c).astype(BF16)


def _rwkv_kernel(p_ref, prev_ref, mu_ref, w0_ref, wup_ref, a0_ref, aup_ref, gup_ref,
                 kk_ref, ka_ref, rk_ref, gng_ref, gnb_ref, o_ref,
                 state_ref, y_s, *prepared):
    tt = p_ref.shape[0]
    c = RW_CHUNK
    nch = tt // c
    w = MIX_W
    npair = w // LANES
    ti = pl.program_id(1)

    @pl.when(ti == 0)
    def _():
        state_ref[...] = jnp.zeros_like(state_ref)

    hsum = _head_sum_matrix(LANES, HEAD_DIM)

    def prepare(kkd_s, rd_s, kinv_s, binv_s, kend_s, bend_s, v_s, g_s, bonus_s, decay_s):
        p = p_ref[...]
        prev = jnp.where(ti == 0, 0.0, prev_ref[7:8, :])
        rowid = lax.broadcasted_iota(jnp.int32, (tt, 1), 0)
        shifted = jnp.where(rowid == 0, prev, pltpu.roll(p, 1, 0))
        xs = p + (shifted - p) * mu_ref[...]
        r = xs[:, 0:w]
        k = xs[:, w:2 * w]
        v = xs[:, 2 * w:3 * w]
        lora_in = xs[:, 3 * w:3 * w + LANES]
        gd = xs[:, 3 * w + LANES:3 * w + 2 * LANES]
        dec = w0_ref[...] + _dot(jnp.tanh(lora_in).astype(BF16), wup_ref[...])
        lw = -jnp.exp(-_softplus(-dec) - 0.5)
        a = _sigmoid(a0_ref[...] + _dot(lora_in.astype(BF16), aup_ref[...]))
        g_s[...] = _dot(_sigmoid(gd).astype(BF16), gup_ref[...])

        def head_sum(x):
            return jnp.concatenate([_dot(x[:, i * LANES:(i + 1) * LANES].astype(BF16), hsum)
                                    for i in range(npair)], axis=1)

        kk = k * kk_ref[...]
        kkn = kk * lax.rsqrt(jnp.maximum(head_sum(kk * kk), 1e-24))
        k2 = k * (1.0 + (a - 1.0) * ka_ref[...])
        bvec = kkn * a
        bonus_s[...] = head_sum(r * k2 * rk_ref[...]) * v
        trow = lax.broadcasted_iota(jnp.int32, (tt, tt), 0)
        tcol = lax.broadcasted_iota(jnp.int32, (tt, tt), 1)
        cum = jnp.logical_and(trow // c == tcol // c, trow >= tcol).astype(BF16)
        lw_hi = lw.astype(BF16)
        lw_lo = (lw - lw_hi.astype(F32)).astype(BF16)
        cs = _dot(cum, lw_hi) + _dot(cum, lw_lo)
        tot = cs.reshape(nch, c, w)[:, c - 1:c, :]
        to_end = (tot - cs.reshape(nch, c, w)).reshape(tt, w)
        p_inv = jnp.exp(-cs)
        p_end = jnp.exp(to_end)
        kkd_s[...] = (kkn * jnp.exp(cs - lw)).astype(BF16)
        rd_s[...] = (r * jnp.exp(cs)).astype(BF16)
        kinv_s[...] = (k2 * p_inv).astype(BF16)
        binv_s[...] = (bvec * p_inv).astype(BF16)
        kend_s[...] = (k2 * p_end).astype(BF16)
        bend_s[...] = (-(bvec * p_end)).astype(BF16)
        v_s[...] = v.astype(BF16)
        decay_s[...] = jnp.exp(tot)

    prepare(*prepared)
    _rwkv_scan_tile(prepared, state_ref, y_s, hsum, gng_ref, gnb_ref, o_ref)


def _rwkv_scan_tile(scratch_set, state_ref, y_s, hsum, gng_ref, gnb_ref, o_ref):
    kkd_s, rd_s, kinv_s, binv_s, kend_s, bend_s, v_s, g_s, bonus_s, decay_s = scratch_set
    tt = y_s.shape[0]
    c = RW_CHUNK
    c2 = 2 * c
    nch = tt // c
    npair = MIX_W // LANES

    def rd(ref, u):
        return ref[u[0] * c:(u[0] + 1) * c, u[1] * LANES:(u[1] + 1) * LANES]

    lane = lax.broadcasted_iota(jnp.int32, (1, LANES), 1)
    m0 = lane < HEAD_DIM
    row = lax.broadcasted_iota(jnp.int32, (c2, c2), 0)
    col = lax.broadcasted_iota(jnp.int32, (c2, c2), 1)
    same_head = (row // c) == (col // c)
    strict = jnp.logical_and(same_head, (row % c) > (col % c))
    incl = jnp.logical_and(same_head, (row % c) >= (col % c))
    eye = (row == col).astype(F32)
    head_block = same_head.astype(F32)
    zb = jnp.zeros((c, LANES), BF16)

    def stack(x):
        return jnp.concatenate([jnp.where(m0, x, zb), jnp.where(m0, zb, x)], axis=0)

    units = [(ci, pi) for ci in range(nch) for pi in range(npair)]
    rows = {u: slice(u[0] * c, (u[0] + 1) * c) for u in units}
    lanes = {u: slice(u[1] * LANES, (u[1] + 1) * LANES) for u in units}
    left = {u: jnp.concatenate([stack(rd(kkd_s, u)), stack(rd(rd_s, u))], axis=0)
            for u in units}
    scores = {u: _dot_nt(left[u], jnp.concatenate([stack(rd(kinv_s, u)), stack(rd(binv_s, u))], axis=0))
              for u in units}
    a_kb = {u: jnp.where(strict, scores[u][0:c2, c2:2 * c2], 0.0) for u in units}
    a_rb = {u: jnp.where(incl, scores[u][c2:2 * c2, c2:2 * c2], 0.0).astype(BF16) for u in units}
    from_v = {u: _dot(jnp.concatenate([jnp.where(strict, scores[u][0:c2, 0:c2], 0.0),
                                       jnp.where(incl, scores[u][c2:2 * c2, 0:c2], 0.0)], axis=0).astype(BF16),
                      stack(rd(v_s, u))) for u in units}
    inv = {u: eye - a_kb[u] for u in units}
    power = {}
    for u in units:
        ab = a_kb[u].astype(BF16)
        power[u] = _dot(ab, ab).astype(BF16)
    for lvl in range(5):
        for u in units:
            if lvl < 4:
                both = _dot(jnp.concatenate([inv[u].astype(BF16), power[u]], axis=0), power[u])
                inv[u] = inv[u] + both[0:c2]
                power[u] = both[c2:].astype(BF16)
            else:
                inv[u] = (inv[u] + _dot(inv[u].astype(BF16), power[u])).astype(BF16)
    states = [state_ref[pi] for pi in range(npair)]
    for ci in range(nch):
        cu = [(ci, pi) for pi in range(npair)]
        from_state = {u: _dot_nt(left[u], states[u[1]].astype(BF16)) for u in cu}
        u_stack = {u: _dot(inv[u], (from_state[u][0:c2] + from_v[u][0:c2]).astype(BF16)) for u in cu}
        y_stack = {u: from_state[u][c2:] + from_v[u][c2:] - _dot(a_rb[u], u_stack[u].astype(BF16)) for u in cu}
        for u in cu:
            uu = u_stack[u][0:c] + u_stack[u][c:]
            vu = jnp.concatenate([rd(v_s, u), uu.astype(BF16)], axis=0)
            kb_end = jnp.concatenate([rd(kend_s, u), rd(bend_s, u)], axis=0)
            states[u[1]] = (states[u[1]] * decay_s[ci, :, lanes[u]]
                            + head_block * _dot_tn(vu, kb_end))
            y_s[rows[u], lanes[u]] = y_stack[u][0:c] + y_stack[u][c:]
    for pi in range(npair):
        state_ref[pi] = states[pi]

    for pi in range(npair):
        ls = slice(pi * LANES, (pi + 1) * LANES)
        y = y_s[:, ls]
        mean = _dot(y.astype(BF16), hsum) * (1.0 / HEAD_DIM)
        yc = y - mean
        var = _dot((yc * yc).astype(BF16), hsum) * (1.0 / HEAD_DIM)
        yn = yc * lax.rsqrt(var + RWKV_GN_EPS) * gng_ref[:, ls] + gnb_ref[:, ls]
        o_ref[:, ls] = ((yn + bonus_s[:, ls]) * g_s[:, ls]).astype(o_ref.dtype)


def _rwkv(p_rw, batch, seq, mu, w0, w_up_pad, a0, a_up_pad, g_up, k_k, k_a, r_k, gn_g, gn_b):
    n = p_rw.shape[1]
    w = MIX_W
    tt = _row_tile(seq, RW_TILE)
    nt = seq // tt
    vec = lambda width: pl.BlockSpec((1, width), lambda b, t: (0, 0))
    mat = pl.BlockSpec((LANES, w), lambda b, t: (0, 0))
    tile_bf16 = pltpu.VMEM((tt, w), BF16)
    tile_f32 = pltpu.VMEM((tt, w), F32)
    scratch_set = [tile_bf16] * 7 + [tile_f32] * 2 + [pltpu.VMEM((tt // RW_CHUNK, 1, w), F32)]
    return pl.pallas_call(
        _rwkv_kernel,
        grid=(batch, nt),
        in_specs=[pl.BlockSpec((tt, n), lambda b, t: (b * nt + t, 0)),
                  pl.BlockSpec((8, n), lambda b, t: (jnp.maximum((b * nt + t) * (tt // 8) - 1, 0), 0)),
                  vec(n), vec(w), mat, vec(w), mat, mat, vec(w), vec(w), vec(w), vec(w), vec(w)],
        out_specs=pl.BlockSpec((tt, w), lambda b, t: (b * nt + t, 0)),
        out_shape=jax.ShapeDtypeStruct((batch * seq, w), BF16),
        scratch_shapes=[pltpu.VMEM((w // LANES, LANES, LANES), F32), tile_f32] + scratch_set,
        compiler_params=_params("parallel", "arbitrary"),
        name="rwkv7",
    )(p_rw, p_rw, mu.reshape(1, n), w0.reshape(1, w), w_up_pad, a0.reshape(1, w), a_up_pad, g_up,
      k_k.reshape(1, w), k_a.reshape(1, w), r_k.reshape(1, w), gn_g.reshape(1, w), gn_b.reshape(1, w))


def _hgrn_kernel(q_ref, f_ref, i_ref, g_ref, lbl_ref, ng_ref, o_ref, state_ref, *, layer):
    hc = HG_CHUNK
    tile = min(HG_TILE, q_ref.shape[0])
    nch = tile // hc
    seq = q_ref.shape[0]
    logits = lbl_ref[...]
    e = jnp.exp(logits - jnp.max(logits, axis=0, keepdims=True))
    probs = e / jnp.sum(e, axis=0, keepdims=True)
    lb = jnp.zeros((1, LANES), F32)
    for j in range(1, layer + 1):
        lb = lb + probs[j:j + 1, :]
    ng = ng_ref[...]
    pos = lax.broadcasted_iota(jnp.int32, (tile, 1), 0) % hc
    pos8 = lax.broadcasted_iota(jnp.int32, (nch, SUBLANES, 1), 1)
    state_ref[...] = jnp.zeros_like(state_ref)

    def tile_body(ti, _):
        rows = pl.ds(pl.multiple_of(ti * tile, tile), tile)
        q = q_ref[rows, :]
        fgate = lb + (1.0 - lb) * _sigmoid(f_ref[rows, :])
        xi = i_ref[rows, :]
        v = xi * _sigmoid(xi)
        k = 1.0 - fgate
        b = jnp.log(fgate)
        for d in (1, 2, 4, 8):
            b = b + jnp.where(pos >= d, pltpu.roll(b, d, 0), 0.0)
        b3 = b.reshape(nch, hc, LANES)
        q3 = q.reshape(nch, hc, LANES)
        k3 = k.reshape(nch, hc, LANES)
        v3 = v.reshape(nch, hc, LANES)
        tgt = b3 * LOG2E
        src = (b3 - jnp.log(k3)) * LOG2E
        halves = [slice(h * SUBLANES, (h + 1) * SUBLANES) for h in range(hc // SUBLANES)]
        o_half = [jnp.zeros((nch, SUBLANES, LANES), F32) for _ in halves]
        for s in range(hc):
            src_s = src[:, s:s + 1, :]
            v_s = v3[:, s:s + 1, :]
            for h, rows_h in enumerate(halves):
                first = h * SUBLANES
                if first + SUBLANES - 1 < s:
                    continue
                arg = tgt[:, rows_h, :] - src_s
                if first < s:
                    arg = jnp.where(pos8 >= s - first, arg, -1e30)
                wgt = jnp.sum(q3[:, rows_h, :] * jnp.exp2(arg), axis=-1, keepdims=True)
                o_half[h] = o_half[h] + wgt * v_s
        o3 = jnp.concatenate(o_half, axis=1)
        b_last = b3[:, hc - 1:hc, :]
        qd = (q3 * jnp.exp(b3)).astype(BF16)
        k_end = (k3 * jnp.exp(b_last - b3)).astype(BF16)
        carry = jnp.exp(b_last)
        vb = v3.astype(BF16)
        updates = [_dot_tn(vb[ch], k_end[ch]) for ch in range(nch)]
        states = [state_ref[...]]
        for ch in range(nch):
            states.append(states[ch] * carry[ch] + updates[ch])
        state_ref[...] = states[nch]
        inter = [_dot_nt(qd[ch], states[ch].astype(BF16)) for ch in range(nch)]
        o = o3.reshape(tile, LANES) + jnp.concatenate(inter, axis=0)
        o = o * lax.rsqrt(jnp.mean(o * o, axis=-1, keepdims=True) + RMS_EPS)
        gt = g_ref[rows, :]
        o_ref[rows, :] = (o * ng * (gt * _sigmoid(gt))).astype(o_ref.dtype)
        return 0

    lax.fori_loop(0, seq // tile, tile_body, 0)


def _hgrn(p_hg, batch, seq, lb_logits, norm_g, layer):
    nh = HGRN_HEADS
    x3 = p_hg.reshape(batch, seq, 4 * MIX_W)
    col = lambda j: pl.BlockSpec((None, seq, LANES), lambda b, h: (b, 0, j * nh + h))
    out = pl.pallas_call(
        functools.partial(_hgrn_kernel, layer=layer),
        grid=(batch, nh),
        in_specs=[col(0), col(1), col(2), col(3),
                  pl.BlockSpec((DEPTH, LANES), lambda b, h: (0, h)),
                  pl.BlockSpec((1, LANES), lambda b, h: (0, h))],
        out_specs=pl.BlockSpec((None, seq, LANES), lambda b, h: (b, 0, h)),
        out_shape=jax.ShapeDtypeStruct((batch, seq, MIX_W), BF16),
        scratch_shapes=[pltpu.VMEM((LANES, LANES), F32)],
        compiler_params=_params("parallel", "parallel"),
        name="hgrn2",
    )(x3, x3, x3, x3, lb_logits, norm_g.reshape(1, MIX_W))
    return out.reshape(batch * seq, MIX_W)


def _merge_kernel(h_ref, ysb_ref, yrw_ref, yhg_ref, gate_ref, wup_ref, wout_ref, g_ref, b_ref, o_ref):
    d = D_MODEL
    tm = h_ref.shape[0]
    sub = min(tm, DENSE_SUBTILE)
    for r0 in range(0, tm, sub):
        rows = slice(r0, r0 + sub)
        merged = None
        for j, y_ref in enumerate((ysb_ref, yrw_ref, yhg_ref)):
            term = gate_ref[rows, j * d:(j + 1) * d].astype(F32) * _dot(y_ref[rows, :], wup_ref[j])
            merged = term if merged is None else merged + term
        x = ALPHA * h_ref[rows, :] + _dot(merged.astype(BF16), wout_ref[...])
        o_ref[rows, :] = _layer_norm(x, g_ref[...], b_ref[...])


def _merge(h, y_sb, y_rw, y_hg, gates, w_up, w_out, g, b):
    m, d = h.shape
    tm = _row_tile(m, 512)
    rows = lambda width: pl.BlockSpec((tm, width), lambda i: (i, 0))
    return pl.pallas_call(
        _merge_kernel,
        grid=(m // tm,),
        in_specs=[rows(d), rows(MIX_W), rows(MIX_W), rows(MIX_W), rows(N_BRANCH * d),
                  pl.BlockSpec((N_BRANCH, MIX_W, d), lambda i: (0, 0, 0)),
                  pl.BlockSpec((d, d), lambda i: (0, 0)),
                  pl.BlockSpec((1, d), lambda i: (0, 0)),
                  pl.BlockSpec((1, d), lambda i: (0, 0))],
        out_specs=rows(d),
        out_shape=jax.ShapeDtypeStruct((m, d), F32),
        compiler_params=_params("parallel"),
        name="merge_out",
    )(h, y_sb, y_rw, y_hg, gates, w_up, w_out, g.reshape(1, d), b.reshape(1, d))


def _ffn_kernel(h_ref, wa_ref, wb_ref, wo_ref, g_ref, b_ref, o_ref):
    tm = h_ref.shape[0]
    sub = min(tm, DENSE_SUBTILE)
    for r0 in range(0, tm, sub):
        h = h_ref[r0:r0 + sub, :]
        hb = h.astype(BF16)
        a = _dot(hb, wa_ref[...])
        act = (a * _sigmoid(a) * _dot(hb, wb_ref[...])).astype(BF16)
        o_ref[r0:r0 + sub, :] = _layer_norm(ALPHA * h + _dot(act, wo_ref[...]), g_ref[...], b_ref[...])


def _ffn(h, w_in, w_out, g, b):
    m, d = h.shape
    tm = _row_tile(m, 512)
    return pl.pallas_call(
        _ffn_kernel,
        grid=(m // tm,),
        in_specs=[pl.BlockSpec((tm, d), lambda i: (i, 0)),
                  pl.BlockSpec((d, D_FF), lambda i: (0, 0)),
                  pl.BlockSpec((d, D_FF), lambda i: (0, 1)),
                  pl.BlockSpec((D_FF, d), lambda i: (0, 0)),
                  pl.BlockSpec((1, d), lambda i: (0, 0)),
                  pl.BlockSpec((1, d), lambda i: (0, 0))],
        out_specs=pl.BlockSpec((tm, d), lambda i: (i, 0)),
        out_shape=jax.ShapeDtypeStruct((m, d), F32),
        compiler_params=_params("parallel"),
        name="ffn",
    )(h, w_in, w_in, w_out, g.reshape(1, d), b.reshape(1, d))


def kernel(x, ln_in_g, ln_in_b, w_in, rwkv_mu, rwkv_w0, rwkv_w_up, rwkv_a0, rwkv_a_up, rwkv_g_up,
           rwkv_k_k, rwkv_k_a, rwkv_r_k, rwkv_ln_g, rwkv_ln_b, hgrn_lb_logits, hgrn_norm_g,
           w_branch_up, w_out, ln1_g, ln1_b, w_ffn_in, w_ffn_out, ln2_g, ln2_b):
    batch, seq, d = x.shape
    m = batch * seq
    w = MIX_W
    offs = (0, 3 * w, 3 * w + RWKV_IN, 3 * w + RWKV_IN + 4 * w, w_in.shape[-1])
    w_in_b = w_in.astype(BF16)
    zeros_lora = jnp.zeros((DEPTH, DECAY_LORA, w), BF16)
    w_up_pad = jnp.concatenate([rwkv_w_up.astype(BF16), zeros_lora], axis=1)
    a_up_pad = jnp.concatenate([zeros_lora, rwkv_a_up.astype(BF16)], axis=1)
    g_up_b = rwkv_g_up.astype(BF16)
    w_branch_b = w_branch_up.astype(BF16)
    w_out_b = w_out.astype(BF16)
    w_ffn_in_b = w_ffn_in.astype(BF16)
    w_ffn_out_b = w_ffn_out.astype(BF16)
    lb_logits = hgrn_lb_logits.astype(F32)

    h = _entry_ln(x.reshape(m, d), ln_in_g, ln_in_b)
    for l in range(DEPTH):
        qkv = _project(h, w_in_b[l, :, offs[0]:offs[1]], BF16)
        p_rw = _project(h, w_in_b[l, :, offs[1]:offs[2]], F32)
        p_hg = _project(h, w_in_b[l, :, offs[2]:offs[3]], F32)
        gates = _project(h, w_in_b[l, :, offs[3]:offs[4]], BF16, gate=True)
        y_sb = _sb_attention(qkv, batch, seq)
        y_rw = _rwkv(p_rw, batch, seq, rwkv_mu[l], rwkv_w0[l], w_up_pad[l], rwkv_a0[l], a_up_pad[l],
                     g_up_b[l], rwkv_k_k[l], rwkv_k_a[l], rwkv_r_k[l], rwkv_ln_g[l], rwkv_ln_b[l])
        y_hg = _hgrn(p_hg, batch, seq, lb_logits, hgrn_norm_g[l], l)
        h = _merge(h, y_sb, y_rw, y_hg, gates, w_branch_b[l], w_out_b[l], ln1_g[l], ln1_b[l])
        h = _ffn(h, w_ffn_in_b[l], w_ffn_out_b[l], ln2_g[l], ln2_b[l])
    return h.reshape(batch, seq, d).astype(x.dtype)
```

```python
import functools

import jax
import jax.numpy as jnp
from jax import lax
from jax.experimental import pallas as pl
from jax.experimental.pallas import tpu as pltpu

F32 = jnp.float32
BF16 = jnp.bfloat16

D_MODEL = 1024
DEPTH = 4
MIX_W = D_MODEL // 2
HEAD_DIM = 64
DECAY_LORA = 64
ICL_LORA = 64
GATE_LORA = 128
RWKV_IN = 3 * MIX_W + DECAY_LORA + ICL_LORA + GATE_LORA
HGRN_EXPAND = 128
HGRN_HEADS = MIX_W // HGRN_EXPAND
N_BRANCH = 3
D_FF = -(-(8 * D_MODEL) // (3 * 256)) * 256
ALPHA = (2 * DEPTH) ** 0.25
LN_EPS = 1e-5
RWKV_GN_EPS = 64e-5
RMS_EPS = 1e-6

LANES = 128
SUBLANES = 8
LOG2E = 1.4426950408889634
SB_BLK = 128
SB_TILE = 512
SB_HEAD_STEPS = 3
SB_SUBSTEPS = 2
SB_DEAD = 104.0
RW_CHUNK = 64
RW_TILE = 256
HG_CHUNK = 16
HG_TILE = 256
DENSE_SUBTILE = 128
VMEM_LIMIT = 56 * 1024 * 1024

_NT = (((1,), (1,)), ((), ()))
_TN = (((0,), (0,)), ((), ()))


def _dot(a, b):
    return jnp.dot(a, b, preferred_element_type=F32)


def _dot_nt(a, b):
    return lax.dot_general(a, b, _NT, preferred_element_type=F32)


def _dot_tn(a, b):
    return lax.dot_general(a, b, _TN, preferred_element_type=F32)


def _dot2(x, w):
    hi = x.astype(BF16)
    lo = (x - hi.astype(F32)).astype(BF16)
    return _dot(hi, w) + _dot(lo, w)


def _sigmoid(x):
    return 1.0 / (1.0 + jnp.exp(-x))


def _softplus(x):
    return jnp.maximum(x, 0.0) + jnp.log(1.0 + jnp.exp(-jnp.abs(x)))


def _softplus2(x):
    return jnp.maximum(x, 0.0) + jnp.log(1.0 + jnp.exp2(-jnp.abs(x))) * LOG2E


def _layer_norm(x, g, b):
    mu = jnp.mean(x, -1, keepdims=True)
    xc = x - mu
    var = jnp.mean(xc * xc, -1, keepdims=True)
    return xc * lax.rsqrt(var + LN_EPS) * g + b


def _div_pow2(x, n):
    assert n & (n - 1) == 0
    return lax.shift_right_logical(x, n.bit_length() - 1)


def _mod_pow2(x, n):
    assert n & (n - 1) == 0
    return x & (n - 1)


def _params(*sem):
    return pltpu.CompilerParams(dimension_semantics=sem, vmem_limit_bytes=VMEM_LIMIT)


def _row_tile(m, want):
    t = min(m, want)
    assert m % t == 0
    return t


def _ln_kernel(x_ref, g_ref, b_ref, o_ref):
    o_ref[...] = _layer_norm(x_ref[...], g_ref[...], b_ref[...])


def _entry_ln(x2, g, b):
    m, d = x2.shape
    tm = _row_tile(m, 512)
    return pl.pallas_call(
        _ln_kernel,
        grid=(m // tm,),
        in_specs=[pl.BlockSpec((tm, d), lambda i: (i, 0)),
                  pl.BlockSpec((1, d), lambda i: (0, 0)),
                  pl.BlockSpec((1, d), lambda i: (0, 0))],
        out_specs=pl.BlockSpec((tm, d), lambda i: (i, 0)),
        out_shape=jax.ShapeDtypeStruct((m, d), F32),
        compiler_params=_params("parallel"),
        name="entry_ln",
    )(x2, g.reshape(1, d), b.reshape(1, d))


PROJ_WIDTHS = (3 * MIX_W, RWKV_IN, 4 * MIX_W, N_BRANCH * D_MODEL)
PROJ_DTYPES = (BF16, F32, F32, BF16)
PROJ_STEP = 256


def _proj_kernel(h_ref, w_ref, *out_refs):
    hb = h_ref[...].astype(BF16)
    off = 0
    for out_ref in out_refs:
        is_gate = out_ref is out_refs[-1]
        for n0 in range(0, out_ref.shape[1], PROJ_STEP):
            y = _dot(hb, w_ref[:, off + n0:off + n0 + PROJ_STEP])
            if is_gate:
                y = _sigmoid(y)
            out_ref[:, n0:n0 + PROJ_STEP] = y.astype(out_ref.dtype)
        off += out_ref.shape[1]


def _project(h, w_all, layer):
    m, d = h.shape
    tm = _row_tile(m, 512)
    return pl.pallas_call(
        _proj_kernel,
        grid=(m // tm,),
        in_specs=[pl.BlockSpec((tm, d), lambda i: (i, 0)),
                  pl.BlockSpec((None, d, w_all.shape[2]), lambda i: (layer, 0, 0),
                               pipeline_mode=pl.Buffered(1))],
        out_specs=[pl.BlockSpec((tm, n), lambda i: (i, 0)) for n in PROJ_WIDTHS],
        out_shape=[jax.ShapeDtypeStruct((m, n), dt) for n, dt in zip(PROJ_WIDTHS, PROJ_DTYPES)],
        compiler_params=_params("parallel"),
        name="in_proj",
    )(h, w_all)


def _sb_kernel(q_ref, k_ref, v_ref, o_ref):
    blk = SB_BLK
    nq = q_ref.shape[0] // blk
    qb0 = pl.program_id(2) * nq
    lane = lax.broadcasted_iota(jnp.int32, (1, LANES), 1)
    m0 = lane < HEAD_DIM
    row = _mod_pow2(lax.broadcasted_iota(jnp.int32, (2 * blk, blk), 0), blk)
    col = lax.broadcasted_iota(jnp.int32, (2 * blk, blk), 1)
    strict = col < row
    krow = lax.broadcasted_iota(jnp.int32, (blk, blk), 0)
    kcol = lax.broadcasted_iota(jnp.int32, (blk, blk), 1)
    later = (krow > kcol).astype(BF16)
    q = (q_ref[...].astype(F32) * (HEAD_DIM ** -0.5 * LOG2E)).astype(BF16)
    zq = jnp.zeros((blk, LANES), BF16)
    qs = [jnp.concatenate([jnp.where(m0, q[j * blk:(j + 1) * blk], zq),
                           jnp.where(m0, zq, q[j * blk:(j + 1) * blk])], axis=0) for j in range(nq)]

    def steps(d, nsub, runs, accs, diagonal):
        units = [(s, j) for s in range(nsub) for j in range(nq)]
        kb = {u: qb0 + u[1] - d - u[0] for u in units}
        start = {u: pl.multiple_of(jnp.maximum(kb[u], 0) * blk, blk) for u in units}
        z = {u: _dot_nt(qs[u[1]], k_ref[pl.ds(start[u], blk), :]) for u in units}
        sp, spm, within = {}, {}, {}
        for u in units:
            sp[u] = _softplus2(z[u])
            spm[u] = jnp.where(strict, sp[u], 0.0) if (diagonal and u[0] == 0) else sp[u]
            within[u] = _dot(spm[u].astype(BF16), later)
        new_runs, new_accs = [], []
        for j in range(nq):
            run = runs[j]
            ws, vs = [], []
            for s in range(nsub):
                u = (s, j)
                on_diagonal = diagonal and s == 0
                if not on_diagonal:
                    run = jnp.where(kb[u] >= 0, run, 1e30)
                w = jnp.exp2((z[u] - sp[u]) - (within[u] + run))
                if on_diagonal:
                    w = jnp.where(strict, w, 0.0)
                vblk = v_ref[pl.ds(start[u], blk), :]
                ws += [w[0:blk], w[blk:]]
                vs += [jnp.where(m0, vblk, zq), jnp.where(m0, zq, vblk)]
                run = run + jnp.sum(spm[u], axis=1, keepdims=True)
            new_accs.append(accs[j] + _dot(jnp.concatenate(ws, axis=1).astype(BF16),
                                           jnp.concatenate(vs, axis=0)))
            new_runs.append(run)
        return new_runs, new_accs

    runs, accs = steps(0, SB_HEAD_STEPS, [jnp.zeros((2 * blk, 1), F32)] * nq,
                       [jnp.zeros((blk, LANES), F32)] * nq, True)

    def alive_after(d, runs):
        alive = jnp.bool_(False)
        for j in range(nq):
            alive = jnp.logical_or(alive, jnp.logical_and(qb0 + j - d - 1 >= 0,
                                                          jnp.min(runs[j]) < SB_DEAD * LOG2E))
        return alive

    def cond(carry):
        return carry[1]

    def body(carry):
        d, _, runs, accs = carry
        new_runs, new_accs = steps(d, SB_SUBSTEPS, runs, accs, False)
        return d + SB_SUBSTEPS, alive_after(d + SB_SUBSTEPS - 1, new_runs), new_runs, new_accs

    init = (jnp.int32(SB_HEAD_STEPS), alive_after(SB_HEAD_STEPS - 1, runs), runs, accs)
    accs = lax.while_loop(cond, body, init)[3]
    for j in range(nq):
        o_ref[j * blk:(j + 1) * blk, :] = accs[j].astype(o_ref.dtype)


def _sb_attention(qkv, batch, seq):
    blk = _row_tile(seq, SB_TILE)
    ncol = MIX_W // LANES
    x3 = qkv.reshape(batch, seq, 3 * MIX_W)
    out = pl.pallas_call(
        _sb_kernel,
        grid=(batch, ncol, seq // blk),
        in_specs=[pl.BlockSpec((None, blk, LANES), lambda b, c, i: (b, i, c)),
                  pl.BlockSpec((None, seq, LANES), lambda b, c, i: (b, 0, ncol + c)),
                  pl.BlockSpec((None, seq, LANES), lambda b, c, i: (b, 0, 2 * ncol + c))],
        out_specs=pl.BlockSpec((None, blk, LANES), lambda b, c, i: (b, i, c)),
        out_shape=jax.ShapeDtypeStruct((batch, seq, MIX_W), BF16),
        compiler_params=_params("parallel", "parallel", "arbitrary"),
        name="stick_breaking",
    )(x3, x3, x3)
    return out.reshape(batch * seq, MIX_W)


def _head_sum_matrix(width, head):
    r = _div_pow2(lax.broadcasted_iota(jnp.int32, (width, width), 0), head)
    c = _div_pow2(lax.broadcasted_iota(jnp.int32, (width, width), 1), head)
    return (r == c).astype(BF16)


def _rwkv_kernel(p_ref, prev_ref, mu_ref, w0_ref, wup_ref, a0_ref, aup_ref, gup_ref,
                 kk_ref, ka_ref, rk_ref, gng_ref, gnb_ref, hsum_ref, cum_ref, o_ref,
                 state_ref, y_s, *prepared):
    tt = p_ref.shape[0]
    c = RW_CHUNK
    nch = tt // c
    w = MIX_W
    npair = w // LANES
    ti = pl.program_id(1)

    @pl.when(ti == 0)
    def _():
        state_ref[...] = jnp.zeros_like(state_ref)

    hsum = hsum_ref[...]

    def prepare(kkd_s, rd_s, kinv_s, binv_s, kend_s, bend_s, v_s, g_s, bonus_s, decay_s):
        p = p_ref[...]
        prev = jnp.where(ti == 0, 0.0, prev_ref[7:8, :])
        rowid = lax.broadcasted_iota(jnp.int32, (tt, 1), 0)
        shifted = jnp.where(rowid == 0, prev, pltpu.roll(p, 1, 0))
        xs = p + (shifted - p) * mu_ref[...]
        r = xs[:, 0:w]
        k = xs[:, w:2 * w]
        v = xs[:, 2 * w:3 * w]
        lora_in = xs[:, 3 * w:3 * w + LANES]
        gd = xs[:, 3 * w + LANES:3 * w + 2 * LANES]
        dec = w0_ref[...] + _dot(jnp.tanh(lora_in).astype(BF16), wup_ref[...])
        lw = -jnp.exp(-_softplus(-dec) - 0.5)
        a = _sigmoid(a0_ref[...] + _dot(lora_in.astype(BF16), aup_ref[...]))
        g_s[...] = _dot(_sigmoid(gd).astype(BF16), gup_ref[...])

        def head_sum(x):
            return jnp.concatenate([_dot(x[:, i * LANES:(i + 1) * LANES].astype(BF16), hsum)
                                    for i in range(npair)], axis=1)

        kk = k * kk_ref[...]
        kkn = kk * lax.rsqrt(jnp.maximum(head_sum(kk * kk), 1e-24))
        k2 = k * (1.0 + (a - 1.0) * ka_ref[...])
        bvec = kkn * a
        bonus_s[...] = head_sum(r * k2 * rk_ref[...]) * v
        lw_hi = lw.astype(BF16)
        lw_lo = (lw - lw_hi.astype(F32)).astype(BF16)
        cs = _dot(cum_ref[...], lw_hi) + _dot(cum_ref[...], lw_lo)
        tot = cs.reshape(nch, c, w)[:, c - 1:c, :]
        to_end = (tot - cs.reshape(nch, c, w)).reshape(tt, w)
        p_inv = jnp.exp(-cs)
        p_end = jnp.exp(to_end)
        kkd_s[...] = (kkn * jnp.exp(cs - lw)).astype(BF16)
        rd_s[...] = (r * jnp.exp(cs)).astype(BF16)
        kinv_s[...] = (k2 * p_inv).astype(BF16)
        binv_s[...] = (bvec * p_inv).astype(BF16)
        kend_s[...] = (k2 * p_end).astype(BF16)
        bend_s[...] = (-(bvec * p_end)).astype(BF16)
        v_s[...] = v.astype(BF16)
        decay_s[...] = jnp.exp(tot)

    prepare(*prepared)
    _rwkv_scan_tile(prepared, state_ref, y_s, hsum, gng_ref, gnb_ref, o_ref)


def _rwkv_scan_tile(scratch_set, state_ref, y_s, hsum, gng_ref, gnb_ref, o_ref):
    kkd_s, rd_s, kinv_s, binv_s, kend_s, bend_s, v_s, g_s, bonus_s, decay_s = scratch_set
    tt = y_s.shape[0]
    c = RW_CHUNK
    c2 = 2 * c
    nch = tt // c
    npair = MIX_W // LANES

    def rd(ref, u):
        return ref[u[0] * c:(u[0] + 1) * c, u[1] * LANES:(u[1] + 1) * LANES]

    lane = lax.broadcasted_iota(jnp.int32, (1, LANES), 1)
    m0 = lane < HEAD_DIM
    row = lax.broadcasted_iota(jnp.int32, (c2, c2), 0)
    col = lax.broadcasted_iota(jnp.int32, (c2, c2), 1)
    same_head = _div_pow2(row, c) == _div_pow2(col, c)
    strict = jnp.logical_and(same_head, _mod_pow2(row, c) > _mod_pow2(col, c))
    incl = jnp.logical_and(same_head, _mod_pow2(row, c) >= _mod_pow2(col, c))
    eye = (row == col).astype(F32)
    head_block = same_head.astype(F32)
    zb = jnp.zeros((c, LANES), BF16)

    def stack(x):
        return jnp.concatenate([jnp.where(m0, x, zb), jnp.where(m0, zb, x)], axis=0)

    units = [(ci, pi) for ci in range(nch) for pi in range(npair)]
    rows = {u: slice(u[0] * c, (u[0] + 1) * c) for u in units}
    lanes = {u: slice(u[1] * LANES, (u[1] + 1) * LANES) for u in units}
    left = {u: jnp.concatenate([stack(rd(kkd_s, u)), stack(rd(rd_s, u))], axis=0)
            for u in units}
    scores = {u: _dot_nt(left[u], jnp.concatenate([stack(rd(kinv_s, u)), stack(rd(binv_s, u))], axis=0))
              for u in units}
    a_kb = {u: jnp.where(strict, scores[u][0:c2, c2:2 * c2], 0.0) for u in units}
    a_rb = {u: jnp.where(incl, scores[u][c2:2 * c2, c2:2 * c2], 0.0).astype(BF16) for u in units}
    from_v = {u: _dot(jnp.concatenate([jnp.where(strict, scores[u][0:c2, 0:c2], 0.0),
                                       jnp.where(incl, scores[u][c2:2 * c2, 0:c2], 0.0)], axis=0).astype(BF16),
                      stack(rd(v_s, u))) for u in units}
    inv = {u: eye - a_kb[u] for u in units}
    power = {}
    for u in units:
        ab = a_kb[u].astype(BF16)
        power[u] = _dot(ab, ab).astype(BF16)
    for lvl in range(5):
        for u in units:
            if lvl < 4:
                both = _dot(jnp.concatenate([inv[u].astype(BF16), power[u]], axis=0), power[u])
                inv[u] = inv[u] + both[0:c2]
                power[u] = both[c2:].astype(BF16)
            else:
                inv[u] = (inv[u] + _dot(inv[u].astype(BF16), power[u])).astype(BF16)
    states = [state_ref[pi] for pi in range(npair)]
    for ci in range(nch):
        cu = [(ci, pi) for pi in range(npair)]
        from_state = {u: _dot_nt(left[u], states[u[1]].astype(BF16)) for u in cu}
        u_stack = {u: _dot(inv[u], (from_state[u][0:c2] + from_v[u][0:c2]).astype(BF16)) for u in cu}
        y_stack = {u: from_state[u][c2:] + from_v[u][c2:] - _dot(a_rb[u], u_stack[u].astype(BF16)) for u in cu}
        for u in cu:
            uu = u_stack[u][0:c] + u_stack[u][c:]
            vu = jnp.concatenate([rd(v_s, u), uu.astype(BF16)], axis=0)
            kb_end = jnp.concatenate([rd(kend_s, u), rd(bend_s, u)], axis=0)
            states[u[1]] = (states[u[1]] * decay_s[ci, :, lanes[u]]
                            + head_block * _dot_tn(vu, kb_end))
            y_s[rows[u], lanes[u]] = y_stack[u][0:c] + y_stack[u][c:]
    for pi in range(npair):
        state_ref[pi] = states[pi]

    for pi in range(npair):
        ls = slice(pi * LANES, (pi + 1) * LANES)
        y = y_s[:, ls]
        mean = _dot(y.astype(BF16), hsum) * (1.0 / HEAD_DIM)
        yc = y - mean
        var = _dot((yc * yc).astype(BF16), hsum) * (1.0 / HEAD_DIM)
        yn = yc * lax.rsqrt(var + RWKV_GN_EPS) * gng_ref[:, ls] + gnb_ref[:, ls]
        o_ref[:, ls] = ((yn + bonus_s[:, ls]) * g_s[:, ls]).astype(o_ref.dtype)


def _rwkv(p_rw, batch, seq, mu, w0, w_up_pad, a0, a_up_pad, g_up, k_k, k_a, r_k, gn_g, gn_b):
    n = p_rw.shape[1]
    w = MIX_W
    tt = _row_tile(seq, RW_TILE)
    nt = seq // tt
    vec = lambda width: pl.BlockSpec((1, width), lambda b, t: (0, 0))
    mat = pl.BlockSpec((LANES, w), lambda b, t: (0, 0))
    tile_bf16 = pltpu.VMEM((tt, w), BF16)
    tile_f32 = pltpu.VMEM((tt, w), F32)
    scratch_set = [tile_bf16] * 7 + [tile_f32] * 2 + [pltpu.VMEM((tt // RW_CHUNK, 1, w), F32)]
    hsum = _head_sum_matrix(LANES, HEAD_DIM)
    trow = lax.broadcasted_iota(jnp.int32, (tt, tt), 0)
    tcol = lax.broadcasted_iota(jnp.int32, (tt, tt), 1)
    cum = jnp.logical_and(trow // RW_CHUNK == tcol // RW_CHUNK, trow >= tcol).astype(BF16)
    whole = lambda a: pl.BlockSpec(a.shape, lambda b, t: (0, 0))
    return pl.pallas_call(
        _rwkv_kernel,
        grid=(batch, nt),
        in_specs=[pl.BlockSpec((tt, n), lambda b, t: (b * nt + t, 0)),
                  pl.BlockSpec((8, n), lambda b, t: (jnp.maximum((b * nt + t) * (tt // 8) - 1, 0), 0)),
                  vec(n), vec(w), mat, vec(w), mat, mat, vec(w), vec(w), vec(w), vec(w), vec(w),
                  whole(hsum), whole(cum)],
        out_specs=pl.BlockSpec((tt, w), lambda b, t: (b * nt + t, 0)),
        out_shape=jax.ShapeDtypeStruct((batch * seq, w), BF16),
        scratch_shapes=[pltpu.VMEM((w // LANES, LANES, LANES), F32), tile_f32] + scratch_set,
        compiler_params=_params("parallel", "arbitrary"),
        name="rwkv7",
    )(p_rw, p_rw, mu.reshape(1, n), w0.reshape(1, w), w_up_pad, a0.reshape(1, w), a_up_pad, g_up,
      k_k.reshape(1, w), k_a.reshape(1, w), r_k.reshape(1, w), gn_g.reshape(1, w), gn_b.reshape(1, w),
      hsum, cum)


def _hgrn_kernel(q_ref, f_ref, i_ref, g_ref, lbl_ref, ng_ref, o_ref, state_ref, *, layer):
    hc = HG_CHUNK
    tile = min(HG_TILE, q_ref.shape[0])
    nch = tile // hc
    seq = q_ref.shape[0]
    logits = lbl_ref[...]
    e = jnp.exp(logits - jnp.max(logits, axis=0, keepdims=True))
    probs = e / jnp.sum(e, axis=0, keepdims=True)
    lb = jnp.zeros((1, LANES), F32)
    for j in range(1, layer + 1):
        lb = lb + probs[j:j + 1, :]
    ng = ng_ref[...]
    pos = _mod_pow2(lax.broadcasted_iota(jnp.int32, (tile, 1), 0), hc)
    pos8 = lax.broadcasted_iota(jnp.int32, (nch, SUBLANES, 1), 1)
    state_ref[...] = jnp.zeros_like(state_ref)

    def tile_body(ti, _):
        rows = pl.ds(pl.multiple_of(ti * tile, tile), tile)
        q = q_ref[rows, :]
        fgate = lb + (1.0 - lb) * _sigmoid(f_ref[rows, :])
        xi = i_ref[rows, :]
        v = xi * _sigmoid(xi)
        k = 1.0 - fgate
        b = jnp.log(fgate)
        for d in (1, 2, 4, 8):
            b = b + jnp.where(pos >= d, pltpu.roll(b, d, 0), 0.0)
        b3 = b.reshape(nch, hc, LANES)
        q3 = q.reshape(nch, hc, LANES)
        k3 = k.reshape(nch, hc, LANES)
        v3 = v.reshape(nch, hc, LANES)
        tgt = b3 * LOG2E
        src = (b3 - jnp.log(k3)) * LOG2E
        halves = [slice(h * SUBLANES, (h + 1) * SUBLANES) for h in range(hc // SUBLANES)]
        o_half = [jnp.zeros((nch, SUBLANES, LANES), F32) for _ in halves]
        for s in range(hc):
            src_s = src[:, s:s + 1, :]
            v_s = v3[:, s:s + 1, :]
            for h, rows_h in enumerate(halves):
                first = h * SUBLANES
                if first + SUBLANES - 1 < s:
                    continue
                arg = tgt[:, rows_h, :] - src_s
                if first < s:
                    arg = jnp.where(pos8 >= s - first, arg, -1e30)
                wgt = jnp.sum(q3[:, rows_h, :] * jnp.exp2(arg), axis=-1, keepdims=True)
                o_half[h] = o_half[h] + wgt * v_s
        o3 = jnp.concatenate(o_half, axis=1)
        b_last = b3[:, hc - 1:hc, :]
        qd = (q3 * jnp.exp(b3)).astype(BF16)
        k_end = (k3 * jnp.exp(b_last - b3)).astype(BF16)
        carry = jnp.exp(b_last)
        vb = v3.astype(BF16)
        updates = [_dot_tn(vb[ch], k_end[ch]) for ch in range(nch)]
        states = [state_ref[...]]
        for ch in range(nch):
            states.append(states[ch] * carry[ch] + updates[ch])
        state_ref[...] = states[nch]
        inter = [_dot_nt(qd[ch], states[ch].astype(BF16)) for ch in range(nch)]
        o = o3.reshape(tile, LANES) + jnp.concatenate(inter, axis=0)
        o = o * lax.rsqrt(jnp.mean(o * o, axis=-1, keepdims=True) + RMS_EPS)
        gt = g_ref[rows, :]
        o_ref[rows, :] = (o * ng * (gt * _sigmoid(gt))).astype(o_ref.dtype)
        return 0

    lax.fori_loop(0, seq // tile, tile_body, 0)


def _hgrn(p_hg, batch, seq, lb_logits, norm_g, layer):
    nh = HGRN_HEADS
    x3 = p_hg.reshape(batch, seq, 4 * MIX_W)
    col = lambda j: pl.BlockSpec((None, seq, LANES), lambda b, h: (b, 0, j * nh + h))
    out = pl.pallas_call(
        functools.partial(_hgrn_kernel, layer=layer),
        grid=(batch, nh),
        in_specs=[col(0), col(1), col(2), col(3),
                  pl.BlockSpec((DEPTH, LANES), lambda b, h: (0, h)),
                  pl.BlockSpec((1, LANES), lambda b, h: (0, h))],
        out_specs=pl.BlockSpec((None, seq, LANES), lambda b, h: (b, 0, h)),
        out_shape=jax.ShapeDtypeStruct((batch, seq, MIX_W), BF16),
        scratch_shapes=[pltpu.VMEM((LANES, LANES), F32)],
        compiler_params=_params("parallel", "parallel"),
        name="hgrn2",
    )(x3, x3, x3, x3, lb_logits, norm_g.reshape(1, MIX_W))
    return out.reshape(batch * seq, MIX_W)


def _merge_kernel(h_ref, ysb_ref, yrw_ref, yhg_ref, gate_ref, wup_ref, wout_ref, g_ref, b_ref, o_ref):
    d = D_MODEL
    tm = h_ref.shape[0]
    sub = min(tm, DENSE_SUBTILE)
    for r0 in range(0, tm, sub):
        rows = slice(r0, r0 + sub)
        merged = None
        for j, y_ref in enumerate((ysb_ref, yrw_ref, yhg_ref)):
            term = gate_ref[rows, j * d:(j + 1) * d].astype(F32) * _dot(y_ref[rows, :], wup_ref[j])
            merged = term if merged is None else merged + term
        x = ALPHA * h_ref[rows, :] + _dot(merged.astype(BF16), wout_ref[...])
        o_ref[rows, :] = _layer_norm(x, g_ref[...], b_ref[...])


def _merge(h, y_sb, y_rw, y_hg, gates, w_up, w_out, g, b):
    m, d = h.shape
    tm = _row_tile(m, 512)
    rows = lambda width: pl.BlockSpec((tm, width), lambda i: (i, 0))
    return pl.pallas_call(
        _merge_kernel,
        grid=(m // tm,),
        in_specs=[rows(d), rows(MIX_W), rows(MIX_W), rows(MIX_W), rows(N_BRANCH * d),
                  pl.BlockSpec((N_BRANCH, MIX_W, d), lambda i: (0, 0, 0)),
                  pl.BlockSpec((d, d), lambda i: (0, 0)),
                  pl.BlockSpec((1, d), lambda i: (0, 0)),
                  pl.BlockSpec((1, d), lambda i: (0, 0))],
        out_specs=rows(d),
        out_shape=jax.ShapeDtypeStruct((m, d), F32),
        compiler_params=_params("parallel"),
        name="merge_out",
    )(h, y_sb, y_rw, y_hg, gates, w_up, w_out, g.reshape(1, d), b.reshape(1, d))


def _ffn_kernel(h_ref, wa_ref, wb_ref, wo_ref, g_ref, b_ref, o_ref):
    tm = h_ref.shape[0]
    sub = min(tm, DENSE_SUBTILE)
    for r0 in range(0, tm, sub):
        h = h_ref[r0:r0 + sub, :]
        hb = h.astype(BF16)
        a = _dot(hb, wa_ref[...])
        act = (a * _sigmoid(a) * _dot(hb, wb_ref[...])).astype(BF16)
        o_ref[r0:r0 + sub, :] = _layer_norm(ALPHA * h + _dot(act, wo_ref[...]), g_ref[...], b_ref[...])


def _ffn(h, w_in, w_out, g, b):
    m, d = h.shape
    tm = _row_tile(m, 512)
    return pl.pallas_call(
        _ffn_kernel,
        grid=(m // tm,),
        in_specs=[pl.BlockSpec((tm, d), lambda i: (i, 0)),
                  pl.BlockSpec((d, D_FF), lambda i: (0, 0)),
                  pl.BlockSpec((d, D_FF), lambda i: (0, 1)),
                  pl.BlockSpec((D_FF, d), lambda i: (0, 0)),
                  pl.BlockSpec((1, d), lambda i: (0, 0)),
                  pl.BlockSpec((1, d), lambda i: (0, 0))],
        out_specs=pl.BlockSpec((tm, d), lambda i: (i, 0)),
        out_shape=jax.ShapeDtypeStruct((m, d), F32),
        compiler_params=_params("parallel"),
        name="ffn",
    )(h, w_in, w_in, w_out, g.reshape(1, d), b.reshape(1, d))


def kernel(x, ln_in_g, ln_in_b, w_in, rwkv_mu, rwkv_w0, rwkv_w_up, rwkv_a0, rwkv_a_up, rwkv_g_up,
           rwkv_k_k, rwkv_k_a, rwkv_r_k, rwkv_ln_g, rwkv_ln_b, hgrn_lb_logits, hgrn_norm_g,
           w_branch_up, w_out, ln1_g, ln1_b, w_ffn_in, w_ffn_out, ln2_g, ln2_b):
    batch, seq, d = x.shape
    m = batch * seq
    w = MIX_W
    assert w_in.shape[-1] == sum(PROJ_WIDTHS)
    w_in_b = w_in.astype(BF16)
    zeros_lora = jnp.zeros((DEPTH, DECAY_LORA, w), BF16)
    w_up_pad = jnp.concatenate([rwkv_w_up.astype(BF16), zeros_lora], axis=1)
    a_up_pad = jnp.concatenate([zeros_lora, rwkv_a_up.astype(BF16)], axis=1)
    g_up_b = rwkv_g_up.astype(BF16)
    w_branch_b = w_branch_up.astype(BF16)
    w_out_b = w_out.astype(BF16)
    w_ffn_in_b = w_ffn_in.astype(BF16)
    w_ffn_out_b = w_ffn_out.astype(BF16)
    lb_logits = hgrn_lb_logits.astype(F32)

    h = _entry_ln(x.reshape(m, d), ln_in_g, ln_in_b)
    for l in range(DEPTH):
        qkv, p_rw, p_hg, gates = _project(h, w_in_b, l)
        y_sb = _sb_attention(qkv, batch, seq)
        y_rw = _rwkv(p_rw, batch, seq, rwkv_mu[l], rwkv_w0[l], w_up_pad[l], rwkv_a0[l], a_up_pad[l],
                     g_up_b[l], rwkv_k_k[l], rwkv_k_a[l], rwkv_r_k[l], rwkv_ln_g[l], rwkv_ln_b[l])
        y_hg = _hgrn(p_hg, batch, seq, lb_logits, hgrn_norm_g[l], l)
        h = _merge(h, y_sb, y_rw, y_hg, gates, w_branch_b[l], w_out_b[l], ln1_g[l], ln1_b[l])
        h = _ffn(h, w_ffn_in_b[l], w_ffn_out_b[l], ln2_g[l], ln2_b[l])
    return h.reshape(batch, seq, d).astype(x.dtype)
```

```python
import functools

import jax
import jax.numpy as jnp
from jax import lax
from jax.experimental import pallas as pl
from jax.experimental.pallas import tpu as pltpu

F32 = jnp.float32
BF16 = jnp.bfloat16

D_MODEL = 1024
DEPTH = 4
MIX_W = D_MODEL // 2
HEAD_DIM = 64
DECAY_LORA = 64
ICL_LORA = 64
GATE_LORA = 128
RWKV_IN = 3 * MIX_W + DECAY_LORA + ICL_LORA + GATE_LORA
HGRN_EXPAND = 128
HGRN_HEADS = MIX_W // HGRN_EXPAND
N_BRANCH = 3
D_FF = -(-(8 * D_MODEL) // (3 * 256)) * 256
ALPHA = (2 * DEPTH) ** 0.25
LN_EPS = 1e-5
RWKV_GN_EPS = 64e-5
RMS_EPS = 1e-6

LANES = 128
SUBLANES = 8
LOG2E = 1.4426950408889634
SB_BLK = 128
SB_TILE = 512
SB_HEAD_STEPS = 3
SB_SUBSTEPS = 2
SB_DEAD = 104.0
RW_CHUNK = 64
RW_TILE = 256
RW_SEQS = 4
HG_CHUNK = 16
HG_TILE = 256
DENSE_SUBTILE = 128
VMEM_LIMIT = 56 * 1024 * 1024

_NT = (((1,), (1,)), ((), ()))
_TN = (((0,), (0,)), ((), ()))


def _dot(a, b):
    return jnp.dot(a, b, preferred_element_type=F32)


def _dot_nt(a, b):
    return lax.dot_general(a, b, _NT, preferred_element_type=F32)


def _dot_tn(a, b):
    return lax.dot_general(a, b, _TN, preferred_element_type=F32)


def _dot2(x, w):
    hi = x.astype(BF16)
    lo = (x - hi.astype(F32)).astype(BF16)
    return _dot(hi, w) + _dot(lo, w)


def _sigmoid(x):
    return 1.0 / (1.0 + jnp.exp(-x))


def _softplus(x):
    return jnp.maximum(x, 0.0) + jnp.log(1.0 + jnp.exp(-jnp.abs(x)))


def _softplus2(x):
    return jnp.maximum(x, 0.0) + jnp.log(1.0 + jnp.exp2(-jnp.abs(x))) * LOG2E


def _layer_norm(x, g, b):
    mu = jnp.mean(x, -1, keepdims=True)
    xc = x - mu
    var = jnp.mean(xc * xc, -1, keepdims=True)
    return xc * lax.rsqrt(var + LN_EPS) * g + b


def _div_pow2(x, n):
    assert n & (n - 1) == 0
    return lax.shift_right_logical(x, n.bit_length() - 1)


def _mod_pow2(x, n):
    assert n & (n - 1) == 0
    return x & (n - 1)


def _params(*sem):
    return pltpu.CompilerParams(dimension_semantics=sem, vmem_limit_bytes=VMEM_LIMIT)


def _row_tile(m, want):
    t = min(m, want)
    assert m % t == 0
    return t


def _ln_kernel(x_ref, g_ref, b_ref, o_ref):
    o_ref[...] = _layer_norm(x_ref[...], g_ref[...], b_ref[...])


def _entry_ln(x2, g, b):
    m, d = x2.shape
    tm = _row_tile(m, 512)
    return pl.pallas_call(
        _ln_kernel,
        grid=(m // tm,),
        in_specs=[pl.BlockSpec((tm, d), lambda i: (i, 0)),
                  pl.BlockSpec((1, d), lambda i: (0, 0)),
                  pl.BlockSpec((1, d), lambda i: (0, 0))],
        out_specs=pl.BlockSpec((tm, d), lambda i: (i, 0)),
        out_shape=jax.ShapeDtypeStruct((m, d), F32),
        compiler_params=_params("parallel"),
        name="entry_ln",
    )(x2, g.reshape(1, d), b.reshape(1, d))


PROJ_WIDTHS = (3 * MIX_W, RWKV_IN, 4 * MIX_W, N_BRANCH * D_MODEL)
PROJ_DTYPES = (BF16, F32, F32, BF16)
PROJ_STEP = 256


def _proj_kernel(h_ref, w_ref, *out_refs):
    hb = h_ref[...].astype(BF16)
    off = 0
    for out_ref in out_refs:
        is_gate = out_ref is out_refs[-1]
        for n0 in range(0, out_ref.shape[1], PROJ_STEP):
            y = _dot(hb, w_ref[:, off + n0:off + n0 + PROJ_STEP])
            if is_gate:
                y = _sigmoid(y)
            out_ref[:, n0:n0 + PROJ_STEP] = y.astype(out_ref.dtype)
        off += out_ref.shape[1]


def _project(h, w_all, layer):
    m, d = h.shape
    tm = _row_tile(m, 512)
    return pl.pallas_call(
        _proj_kernel,
        grid=(m // tm,),
        in_specs=[pl.BlockSpec((tm, d), lambda i: (i, 0)),
                  pl.BlockSpec((None, d, w_all.shape[2]), lambda i: (layer, 0, 0),
                               pipeline_mode=pl.Buffered(1))],
        out_specs=[pl.BlockSpec((tm, n), lambda i: (i, 0)) for n in PROJ_WIDTHS],
        out_shape=[jax.ShapeDtypeStruct((m, n), dt) for n, dt in zip(PROJ_WIDTHS, PROJ_DTYPES)],
        compiler_params=_params("parallel"),
        name="in_proj",
    )(h, w_all)


def _sb_kernel(q_ref, k_ref, v_ref, o_ref):
    blk = SB_BLK
    nq = q_ref.shape[0] // blk
    qb0 = pl.program_id(2) * nq
    lane = lax.broadcasted_iota(jnp.int32, (1, LANES), 1)
    m0 = lane < HEAD_DIM
    row = _mod_pow2(lax.broadcasted_iota(jnp.int32, (2 * blk, blk), 0), blk)
    col = lax.broadcasted_iota(jnp.int32, (2 * blk, blk), 1)
    strict = col < row
    krow = lax.broadcasted_iota(jnp.int32, (blk, blk), 0)
    kcol = lax.broadcasted_iota(jnp.int32, (blk, blk), 1)
    later = (krow > kcol).astype(BF16)
    q = (q_ref[...].astype(F32) * (HEAD_DIM ** -0.5 * LOG2E)).astype(BF16)
    zq = jnp.zeros((blk, LANES), BF16)
    qs = [jnp.concatenate([jnp.where(m0, q[j * blk:(j + 1) * blk], zq),
                           jnp.where(m0, zq, q[j * blk:(j + 1) * blk])], axis=0) for j in range(nq)]

    def steps(d, nsub, runs, accs, diagonal):
        units = [(s, j) for s in range(nsub) for j in range(nq)]
        kb = {u: qb0 + u[1] - d - u[0] for u in units}
        start = {u: pl.multiple_of(jnp.maximum(kb[u], 0) * blk, blk) for u in units}
        z = {u: _dot_nt(qs[u[1]], k_ref[pl.ds(start[u], blk), :]) for u in units}
        sp, spm, within = {}, {}, {}
        for u in units:
            sp[u] = _softplus2(z[u])
            spm[u] = jnp.where(strict, sp[u], 0.0) if (diagonal and u[0] == 0) else sp[u]
            within[u] = _dot(spm[u].astype(BF16), later)
        new_runs, new_accs = [], []
        for j in range(nq):
            run = runs[j]
            ws, vs = [], []
            for s in range(nsub):
                u = (s, j)
                on_diagonal = diagonal and s == 0
                if not on_diagonal:
                    run = jnp.where(kb[u] >= 0, run, 1e30)
                w = jnp.exp2((z[u] - sp[u]) - (within[u] + run))
                if on_diagonal:
                    w = jnp.where(strict, w, 0.0)
                vblk = v_ref[pl.ds(start[u], blk), :]
                ws += [w[0:blk], w[blk:]]
                vs += [jnp.where(m0, vblk, zq), jnp.where(m0, zq, vblk)]
                run = run + jnp.sum(spm[u], axis=1, keepdims=True)
            new_accs.append(accs[j] + _dot(jnp.concatenate(ws, axis=1).astype(BF16),
                                           jnp.concatenate(vs, axis=0)))
            new_runs.append(run)
        return new_runs, new_accs

    runs, accs = steps(0, SB_HEAD_STEPS, [jnp.zeros((2 * blk, 1), F32)] * nq,
                       [jnp.zeros((blk, LANES), F32)] * nq, True)

    def alive_after(d, runs):
        alive = jnp.bool_(False)
        for j in range(nq):
            alive = jnp.logical_or(alive, jnp.logical_and(qb0 + j - d - 1 >= 0,
                                                          jnp.min(runs[j]) < SB_DEAD * LOG2E))
        return alive

    def cond(carry):
        return carry[1]

    def body(carry):
        d, _, runs, accs = carry
        new_runs, new_accs = steps(d, SB_SUBSTEPS, runs, accs, False)
        return d + SB_SUBSTEPS, alive_after(d + SB_SUBSTEPS - 1, new_runs), new_runs, new_accs

    init = (jnp.int32(SB_HEAD_STEPS), alive_after(SB_HEAD_STEPS - 1, runs), runs, accs)
    accs = lax.while_loop(cond, body, init)[3]
    for j in range(nq):
        o_ref[j * blk:(j + 1) * blk, :] = accs[j].astype(o_ref.dtype)


def _sb_attention(qkv, batch, seq):
    blk = _row_tile(seq, SB_TILE)
    ncol = MIX_W // LANES
    x3 = qkv.reshape(batch, seq, 3 * MIX_W)
    out = pl.pallas_call(
        _sb_kernel,
        grid=(batch, ncol, seq // blk),
        in_specs=[pl.BlockSpec((None, blk, LANES), lambda b, c, i: (b, i, c)),
                  pl.BlockSpec((None, seq, LANES), lambda b, c, i: (b, 0, ncol + c)),
                  pl.BlockSpec((None, seq, LANES), lambda b, c, i: (b, 0, 2 * ncol + c))],
        out_specs=pl.BlockSpec((None, blk, LANES), lambda b, c, i: (b, i, c)),
        out_shape=jax.ShapeDtypeStruct((batch, seq, MIX_W), BF16),
        compiler_params=_params("parallel", "parallel", "arbitrary"),
        name="stick_breaking",
    )(x3, x3, x3)
    return out.reshape(batch * seq, MIX_W)


def _head_sum_matrix(width, head):
    r = _div_pow2(lax.broadcasted_iota(jnp.int32, (width, width), 0), head)
    c = _div_pow2(lax.broadcasted_iota(jnp.int32, (width, width), 1), head)
    return (r == c).astype(BF16)


def _rwkv_kernel(p_ref, prev_ref, mu_ref, w0_ref, wup_ref, a0_ref, aup_ref, gup_ref,
                 kk_ref, ka_ref, rk_ref, gng_ref, gnb_ref, hsum_ref, cum_ref, o_ref,
                 state_ref, y_s, *prepared):
    nseq, tt = p_ref.shape[0], p_ref.shape[1]
    c = RW_CHUNK
    nch = tt // c
    w = MIX_W
    npair = w // LANES
    ti = pl.program_id(1)

    @pl.when(ti == 0)
    def _():
        state_ref[...] = jnp.zeros_like(state_ref)

    hsum = hsum_ref[...]

    def prepare(bi, kkd_s, rd_s, kinv_s, binv_s, kend_s, bend_s, v_s, g_s, bonus_s, decay_s):
        p = p_ref[bi]
        prev = jnp.where(ti == 0, 0.0, prev_ref[bi, 7:8, :])
        rowid = lax.broadcasted_iota(jnp.int32, (tt, 1), 0)
        shifted = jnp.where(rowid == 0, prev, pltpu.roll(p, 1, 0))
        xs = p + (shifted - p) * mu_ref[...]
        r = xs[:, 0:w]
        k = xs[:, w:2 * w]
        v = xs[:, 2 * w:3 * w]
        lora_in = xs[:, 3 * w:3 * w + LANES]
        gd = xs[:, 3 * w + LANES:3 * w + 2 * LANES]
        dec = w0_ref[...] + _dot(jnp.tanh(lora_in).astype(BF16), wup_ref[...])
        lw = -jnp.exp(-_softplus(-dec) - 0.5)
        a = _sigmoid(a0_ref[...] + _dot(lora_in.astype(BF16), aup_ref[...]))
        g_s[bi] = _dot(_sigmoid(gd).astype(BF16), gup_ref[...])

        def head_sum(x):
            return jnp.concatenate([_dot(x[:, i * LANES:(i + 1) * LANES].astype(BF16), hsum)
                                    for i in range(npair)], axis=1)

        kk = k * kk_ref[...]
        kkn = kk * lax.rsqrt(jnp.maximum(head_sum(kk * kk), 1e-24))
        k2 = k * (1.0 + (a - 1.0) * ka_ref[...])
        bvec = kkn * a
        bonus_s[bi] = head_sum(r * k2 * rk_ref[...]) * v
        lw_hi = lw.astype(BF16)
        lw_lo = (lw - lw_hi.astype(F32)).astype(BF16)
        cs = _dot(cum_ref[...], lw_hi) + _dot(cum_ref[...], lw_lo)
        tot = cs.reshape(nch, c, w)[:, c - 1:c, :]
        to_end = (tot - cs.reshape(nch, c, w)).reshape(tt, w)
        p_inv = jnp.exp(-cs)
        p_end = jnp.exp(to_end)
        kkd_s[bi] = (kkn * jnp.exp(cs - lw)).astype(BF16)
        rd_s[bi] = (r * jnp.exp(cs)).astype(BF16)
        kinv_s[bi] = (k2 * p_inv).astype(BF16)
        binv_s[bi] = (bvec * p_inv).astype(BF16)
        kend_s[bi] = (k2 * p_end).astype(BF16)
        bend_s[bi] = (-(bvec * p_end)).astype(BF16)
        v_s[bi] = v.astype(BF16)
        decay_s[bi] = jnp.exp(tot)

    _rwkv_scan_tile(lambda bi: prepare(bi, *prepared), prepared, state_ref, y_s, hsum, gng_ref, gnb_ref, o_ref)


def _rwkv_scan_tile(prepare_fn, scratch_set, state_ref, y_s, hsum, gng_ref, gnb_ref, o_ref):
    kkd_s, rd_s, kinv_s, binv_s, kend_s, bend_s, v_s, g_s, bonus_s, decay_s = scratch_set
    nseq, tt = y_s.shape[0], y_s.shape[1]
    c = RW_CHUNK
    c2 = 2 * c
    nch = tt // c
    npair = MIX_W // LANES

    def rd(ref, u):
        return ref[u[0], u[1] * c:(u[1] + 1) * c, u[2] * LANES:(u[2] + 1) * LANES]

    lane = lax.broadcasted_iota(jnp.int32, (1, LANES), 1)
    m0 = lane < HEAD_DIM
    row = lax.broadcasted_iota(jnp.int32, (c2, c2), 0)
    col = lax.broadcasted_iota(jnp.int32, (c2, c2), 1)
    same_head = _div_pow2(row, c) == _div_pow2(col, c)
    strict = jnp.logical_and(same_head, _mod_pow2(row, c) > _mod_pow2(col, c))
    incl = jnp.logical_and(same_head, _mod_pow2(row, c) >= _mod_pow2(col, c))
    eye = (row == col).astype(F32)
    head_block = same_head.astype(F32)
    zb = jnp.zeros((c, LANES), BF16)

    def stack(x):
        return jnp.concatenate([jnp.where(m0, x, zb), jnp.where(m0, zb, x)], axis=0)

    units = [(bi, ci, pi) for ci in range(nch) for bi in range(nseq) for pi in range(npair)]
    rows = {u: slice(u[1] * c, (u[1] + 1) * c) for u in units}
    lanes = {u: slice(u[2] * LANES, (u[2] + 1) * LANES) for u in units}
    chain = {u: u[0] * npair + u[2] for u in units}
    left, a_rb, from_v, inv = {}, {}, {}, {}
    for bi in range(nseq):
        prepare_fn(bi)
        ub = [u for u in units if u[0] == bi]
        scores, a_kb, power = {}, {}, {}
        for u in ub:
            left[u] = jnp.concatenate([stack(rd(kkd_s, u)), stack(rd(rd_s, u))], axis=0)
        for u in ub:
            scores[u] = _dot_nt(left[u], jnp.concatenate([stack(rd(kinv_s, u)), stack(rd(binv_s, u))], axis=0))
        for u in ub:
            a_kb[u] = jnp.where(strict, scores[u][0:c2, c2:2 * c2], 0.0)
            a_rb[u] = jnp.where(incl, scores[u][c2:2 * c2, c2:2 * c2], 0.0).astype(BF16)
        for u in ub:
            from_v[u] = _dot(jnp.concatenate([jnp.where(strict, scores[u][0:c2, 0:c2], 0.0),
                                              jnp.where(incl, scores[u][c2:2 * c2, 0:c2], 0.0)],
                                             axis=0).astype(BF16), stack(rd(v_s, u)))
        for u in ub:
            inv[u] = eye - a_kb[u]
            ab = a_kb[u].astype(BF16)
            power[u] = _dot(ab, ab).astype(BF16)
        for lvl in range(5):
            for u in ub:
                if lvl < 4:
                    both = _dot(jnp.concatenate([inv[u].astype(BF16), power[u]], axis=0), power[u])
                    inv[u] = inv[u] + both[0:c2]
                    power[u] = both[c2:].astype(BF16)
                else:
                    inv[u] = (inv[u] + _dot(inv[u].astype(BF16), power[u])).astype(BF16)
    states = [state_ref[i] for i in range(nseq * npair)]
    for ci in range(nch):
        cu = [u for u in units if u[1] == ci]
        from_state = {u: _dot_nt(left[u], states[chain[u]].astype(BF16)) for u in cu}
        u_stack = {u: _dot(inv[u], (from_state[u][0:c2] + from_v[u][0:c2]).astype(BF16)) for u in cu}
        y_stack = {u: from_state[u][c2:] + from_v[u][c2:] - _dot(a_rb[u], u_stack[u].astype(BF16)) for u in cu}
        for u in cu:
            uu = u_stack[u][0:c] + u_stack[u][c:]
            vu = jnp.concatenate([rd(v_s, u), uu.astype(BF16)], axis=0)
            kb_end = jnp.concatenate([rd(kend_s, u), rd(bend_s, u)], axis=0)
            states[chain[u]] = (states[chain[u]] * decay_s[u[0], ci, :, lanes[u]]
                                + head_block * _dot_tn(vu, kb_end))
            y_s[u[0], rows[u], lanes[u]] = y_stack[u][0:c] + y_stack[u][c:]
    for i in range(nseq * npair):
        state_ref[i] = states[i]

    for bi in range(nseq):
        for pi in range(npair):
            ls = slice(pi * LANES, (pi + 1) * LANES)
            y = y_s[bi, :, ls]
            mean = _dot(y.astype(BF16), hsum) * (1.0 / HEAD_DIM)
            yc = y - mean
            var = _dot((yc * yc).astype(BF16), hsum) * (1.0 / HEAD_DIM)
            yn = yc * lax.rsqrt(var + RWKV_GN_EPS) * gng_ref[:, ls] + gnb_ref[:, ls]
            o_ref[bi, :, ls] = ((yn + bonus_s[bi, :, ls]) * g_s[bi, :, ls]).astype(o_ref.dtype)


def _rwkv(p_rw, batch, seq, mu, w0, w_up_pad, a0, a_up_pad, g_up, k_k, k_a, r_k, gn_g, gn_b):
    n = p_rw.shape[1]
    w = MIX_W
    tt = _row_tile(seq, RW_TILE)
    nt = seq // tt
    vec = lambda width: pl.BlockSpec((1, width), lambda b, t: (0, 0))
    mat = pl.BlockSpec((LANES, w), lambda b, t: (0, 0))
    ns = RW_SEQS if batch % RW_SEQS == 0 else 1
    tile_bf16 = pltpu.VMEM((ns, tt, w), BF16)
    tile_f32 = pltpu.VMEM((ns, tt, w), F32)
    scratch_set = [tile_bf16] * 7 + [tile_f32] * 2 + [pltpu.VMEM((ns, tt // RW_CHUNK, 1, w), F32)]
    p3 = p_rw.reshape(batch, seq, n)
    hsum = _head_sum_matrix(LANES, HEAD_DIM)
    trow = lax.broadcasted_iota(jnp.int32, (tt, tt), 0)
    tcol = lax.broadcasted_iota(jnp.int32, (tt, tt), 1)
    cum = jnp.logical_and(trow // RW_CHUNK == tcol // RW_CHUNK, trow >= tcol).astype(BF16)
    whole = lambda a: pl.BlockSpec(a.shape, lambda b, t: (0, 0))
    out = pl.pallas_call(
        _rwkv_kernel,
        grid=(batch // ns, nt),
        in_specs=[pl.BlockSpec((ns, tt, n), lambda b, t: (b, t, 0)),
                  pl.BlockSpec((ns, 8, n), lambda b, t: (b, jnp.maximum(t * (tt // 8) - 1, 0), 0)),
                  vec(n), vec(w), mat, vec(w), mat, mat, vec(w), vec(w), vec(w), vec(w), vec(w),
                  whole(hsum), whole(cum)],
        out_specs=pl.BlockSpec((ns, tt, w), lambda b, t: (b, t, 0)),
        out_shape=jax.ShapeDtypeStruct((batch, seq, w), BF16),
        scratch_shapes=[pltpu.VMEM((ns * (w // LANES), LANES, LANES), F32), tile_f32] + scratch_set,
        compiler_params=_params("parallel", "arbitrary"),
        name="rwkv7",
    )(p3, p3, mu.reshape(1, n), w0.reshape(1, w), w_up_pad, a0.reshape(1, w), a_up_pad, g_up,
      k_k.reshape(1, w), k_a.reshape(1, w), r_k.reshape(1, w), gn_g.reshape(1, w), gn_b.reshape(1, w),
      hsum, cum)
    return out.reshape(batch * seq, w)


def _hgrn_kernel(q_ref, f_ref, i_ref, g_ref, lbl_ref, ng_ref, o_ref, state_ref, *, layer):
    hc = HG_CHUNK
    tile = min(HG_TILE, q_ref.shape[0])
    nch = tile // hc
    seq = q_ref.shape[0]
    logits = lbl_ref[...]
    e = jnp.exp(logits - jnp.max(logits, axis=0, keepdims=True))
    probs = e / jnp.sum(e, axis=0, keepdims=True)
    lb = jnp.zeros((1, LANES), F32)
    for j in range(1, layer + 1):
        lb = lb + probs[j:j + 1, :]
    ng = ng_ref[...]
    pos = _mod_pow2(lax.broadcasted_iota(jnp.int32, (tile, 1), 0), hc)
    pos8 = lax.broadcasted_iota(jnp.int32, (nch, SUBLANES, 1), 1)
    state_ref[...] = jnp.zeros_like(state_ref)

    def tile_body(ti, _):
        rows = pl.ds(pl.multiple_of(ti * tile, tile), tile)
        q = q_ref[rows, :]
        fgate = lb + (1.0 - lb) * _sigmoid(f_ref[rows, :])
        xi = i_ref[rows, :]
        v = xi * _sigmoid(xi)
        k = 1.0 - fgate
        b = jnp.log(fgate)
        for d in (1, 2, 4, 8):
            b = b + jnp.where(pos >= d, pltpu.roll(b, d, 0), 0.0)
        b3 = b.reshape(nch, hc, LANES)
        q3 = q.reshape(nch, hc, LANES)
        k3 = k.reshape(nch, hc, LANES)
        v3 = v.reshape(nch, hc, LANES)
        tgt = b3 * LOG2E
        src = (b3 - jnp.log(k3)) * LOG2E
        halves = [slice(h * SUBLANES, (h + 1) * SUBLANES) for h in range(hc // SUBLANES)]
        o_half = [jnp.zeros((nch, SUBLANES, LANES), F32) for _ in halves]
        for s in range(hc):
            src_s = src[:, s:s + 1, :]
            v_s = v3[:, s:s + 1, :]
            for h, rows_h in enumerate(halves):
                first = h * SUBLANES
                if first + SUBLANES - 1 < s:
                    continue
                arg = tgt[:, rows_h, :] - src_s
                if first < s:
                    arg = jnp.where(pos8 >= s - first, arg, -1e30)
                wgt = jnp.sum(q3[:, rows_h, :] * jnp.exp2(arg), axis=-1, keepdims=True)
                o_half[h] = o_half[h] + wgt * v_s
        o3 = jnp.concatenate(o_half, axis=1)
        b_last = b3[:, hc - 1:hc, :]
        qd = (q3 * jnp.exp(b3)).astype(BF16)
        k_end = (k3 * jnp.exp(b_last - b3)).astype(BF16)
        carry = jnp.exp(b_last)
        vb = v3.astype(BF16)
        updates = [_dot_tn(vb[ch], k_end[ch]) for ch in range(nch)]
        states = [state_ref[...]]
        for ch in range(nch):
            states.append(states[ch] * carry[ch] + updates[ch])
        state_ref[...] = states[nch]
        inter = [_dot_nt(qd[ch], states[ch].astype(BF16)) for ch in range(nch)]
        o = o3.reshape(tile, LANES) + jnp.concatenate(inter, axis=0)
        o = o * lax.rsqrt(jnp.mean(o * o, axis=-1, keepdims=True) + RMS_EPS)
        gt = g_ref[rows, :]
        o_ref[rows, :] = (o * ng * (gt * _sigmoid(gt))).astype(o_ref.dtype)
        return 0

    lax.fori_loop(0, seq // tile, tile_body, 0)


def _hgrn(p_hg, batch, seq, lb_logits, norm_g, layer):
    nh = HGRN_HEADS
    x3 = p_hg.reshape(batch, seq, 4 * MIX_W)
    col = lambda j: pl.BlockSpec((None, seq, LANES), lambda b, h: (b, 0, j * nh + h))
    out = pl.pallas_call(
        functools.partial(_hgrn_kernel, layer=layer),
        grid=(batch, nh),
        in_specs=[col(0), col(1), col(2), col(3),
                  pl.BlockSpec((DEPTH, LANES), lambda b, h: (0, h)),
                  pl.BlockSpec((1, LANES), lambda b, h: (0, h))],
        out_specs=pl.BlockSpec((None, seq, LANES), lambda b, h: (b, 0, h)),
        out_shape=jax.ShapeDtypeStruct((batch, seq, MIX_W), BF16),
        scratch_shapes=[pltpu.VMEM((LANES, LANES), F32)],
        compiler_params=_params("parallel", "parallel"),
        name="hgrn2",
    )(x3, x3, x3, x3, lb_logits, norm_g.reshape(1, MIX_W))
    return out.reshape(batch * seq, MIX_W)


def _merge_kernel(h_ref, ysb_ref, yrw_ref, yhg_ref, gate_ref, wup_ref, wout_ref, g_ref, b_ref, o_ref):
    d = D_MODEL
    tm = h_ref.shape[0]
    sub = min(tm, DENSE_SUBTILE)
    for r0 in range(0, tm, sub):
        rows = slice(r0, r0 + sub)
        merged = None
        for j, y_ref in enumerate((ysb_ref, yrw_ref, yhg_ref)):
            term = gate_ref[rows, j * d:(j + 1) * d].astype(F32) * _dot(y_ref[rows, :], wup_ref[j])
            merged = term if merged is None else merged + term
        x = ALPHA * h_ref[rows, :] + _dot(merged.astype(BF16), wout_ref[...])
        o_ref[rows, :] = _layer_norm(x, g_ref[...], b_ref[...])


def _merge(h, y_sb, y_rw, y_hg, gates, w_up, w_out, g, b):
    m, d = h.shape
    tm = _row_tile(m, 512)
    rows = lambda width: pl.BlockSpec((tm, width), lambda i: (i, 0))
    return pl.pallas_call(
        _merge_kernel,
        grid=(m // tm,),
        in_specs=[rows(d), rows(MIX_W), rows(MIX_W), rows(MIX_W), rows(N_BRANCH * d),
                  pl.BlockSpec((N_BRANCH, MIX_W, d), lambda i: (0, 0, 0)),
                  pl.BlockSpec((d, d), lambda i: (0, 0)),
                  pl.BlockSpec((1, d), lambda i: (0, 0)),
                  pl.BlockSpec((1, d), lambda i: (0, 0))],
        out_specs=rows(d),
        out_shape=jax.ShapeDtypeStruct((m, d), F32),
        compiler_params=_params("parallel"),
        name="merge_out",
    )(h, y_sb, y_rw, y_hg, gates, w_up, w_out, g.reshape(1, d), b.reshape(1, d))


def _ffn_kernel(h_ref, wa_ref, wb_ref, wo_ref, g_ref, b_ref, o_ref):
    tm = h_ref.shape[0]
    sub = min(tm, DENSE_SUBTILE)
    for r0 in range(0, tm, sub):
        h = h_ref[r0:r0 + sub, :]
        hb = h.astype(BF16)
        a = _dot(hb, wa_ref[...])
        act = (a * _sigmoid(a) * _dot(hb, wb_ref[...])).astype(BF16)
        o_ref[r0:r0 + sub, :] = _layer_norm(ALPHA * h + _dot(act, wo_ref[...]), g_ref[...], b_ref[...])


def _ffn(h, w_in, w_out, g, b):
    m, d = h.shape
    tm = _row_tile(m, 512)
    return pl.pallas_call(
        _ffn_kernel,
        grid=(m // tm,),
        in_specs=[pl.BlockSpec((tm, d), lambda i: (i, 0)),
                  pl.BlockSpec((d, D_FF), lambda i: (0, 0)),
                  pl.BlockSpec((d, D_FF), lambda i: (0, 1)),
                  pl.BlockSpec((D_FF, d), lambda i: (0, 0)),
                  pl.BlockSpec((1, d), lambda i: (0, 0)),
                  pl.BlockSpec((1, d), lambda i: (0, 0))],
        out_specs=pl.BlockSpec((tm, d), lambda i: (i, 0)),
        out_shape=jax.ShapeDtypeStruct((m, d), F32),
        compiler_params=_params("parallel"),
        name="ffn",
    )(h, w_in, w_in, w_out, g.reshape(1, d), b.reshape(1, d))


def kernel(x, ln_in_g, ln_in_b, w_in, rwkv_mu, rwkv_w0, rwkv_w_up, rwkv_a0, rwkv_a_up, rwkv_g_up,
           rwkv_k_k, rwkv_k_a, rwkv_r_k, rwkv_ln_g, rwkv_ln_b, hgrn_lb_logits, hgrn_norm_g,
           w_branch_up, w_out, ln1_g, ln1_b, w_ffn_in, w_ffn_out, ln2_g, ln2_b):
    batch, seq, d = x.shape
    m = batch * seq
    w = MIX_W
    assert w_in.shape[-1] == sum(PROJ_WIDTHS)
    w_in_b = w_in.astype(BF16)
    zeros_lora = jnp.zeros((DEPTH, DECAY_LORA, w), BF16)
    w_up_pad = jnp.concatenate([rwkv_w_up.astype(BF16), zeros_lora], axis=1)
    a_up_pad = jnp.concatenate([zeros_lora, rwkv_a_up.astype(BF16)], axis=1)
    g_up_b = rwkv_g_up.astype(BF16)
    w_branch_b = w_branch_up.astype(BF16)
    w_out_b = w_out.astype(BF16)
    w_ffn_in_b = w_ffn_in.astype(BF16)
    w_ffn_out_b = w_ffn_out.astype(BF16)
    lb_logits = hgrn_lb_logits.astype(F32)

    h = _entry_ln(x.reshape(m, d), ln_in_g, ln_in_b)
    for l in range(DEPTH):
        qkv, p_rw, p_hg, gates = _project(h, w_in_b, l)
        y_sb = _sb_attention(qkv, batch, seq)
        y_rw = _rwkv(p_rw, batch, seq, rwkv_mu[l], rwkv_w0[l], w_up_pad[l], rwkv_a0[l], a_up_pad[l],
                     g_up_b[l], rwkv_k_k[l], rwkv_k_a[l], rwkv_r_k[l], rwkv_ln_g[l], rwkv_ln_b[l])
        y_hg = _hgrn(p_hg, batch, seq, lb_logits, hgrn_norm_g[l], l)
        h = _merge(h, y_sb, y_rw, y_hg, gates, w_branch_b[l], w_out_b[l], ln1_g[l], ln1_b[l])
        h = _ffn(h, w_ffn_in_b[l], w_ffn_out_b[l], ln2_g[l], ln2_b[l])
    return h.reshape(batch, seq, d).astype(x.dtype)
```

```python
import functools

import jax
import jax.numpy as jnp
from jax import lax
from jax.experimental import pallas as pl
from jax.experimental.pallas import tpu as pltpu

F32 = jnp.float32
BF16 = jnp.bfloat16

D_MODEL = 1024
DEPTH = 4
MIX_W = D_MODEL // 2
HEAD_DIM = 64
DECAY_LORA = 64
ICL_LORA = 64
GATE_LORA = 128
RWKV_IN = 3 * MIX_W + DECAY_LORA + ICL_LORA + GATE_LORA
HGRN_EXPAND = 128
HGRN_HEADS = MIX_W // HGRN_EXPAND
N_BRANCH = 3
D_FF = -(-(8 * D_MODEL) // (3 * 256)) * 256
ALPHA = (2 * DEPTH) ** 0.25
LN_EPS = 1e-5
RWKV_GN_EPS = 64e-5
RMS_EPS = 1e-6

LANES = 128
SUBLANES = 8
LOG2E = 1.4426950408889634
SB_BLK = 128
SB_TILE = 512
SB_HEAD_STEPS = 3
SB_SUBSTEPS = 2
SB_DEAD = 104.0
RW_CHUNK = 64
RW_TILE = 256
RW_SEQS = 4
HG_CHUNK = 16
HG_TILE = 1024
DENSE_SUBTILE = 128
VMEM_LIMIT = 56 * 1024 * 1024

_NT = (((1,), (1,)), ((), ()))
_TN = (((0,), (0,)), ((), ()))


def _dot(a, b):
    return jnp.dot(a, b, preferred_element_type=F32)


def _dot_nt(a, b):
    return lax.dot_general(a, b, _NT, preferred_element_type=F32)


def _dot_tn(a, b):
    return lax.dot_general(a, b, _TN, preferred_element_type=F32)


def _dot2(x, w):
    hi = x.astype(BF16)
    lo = (x - hi.astype(F32)).astype(BF16)
    return _dot(hi, w) + _dot(lo, w)


def _sigmoid(x):
    return 1.0 / (1.0 + jnp.exp(-x))


def _softplus(x):
    return jnp.maximum(x, 0.0) + jnp.log(1.0 + jnp.exp(-jnp.abs(x)))


def _softplus2(x):
    return jnp.maximum(x, 0.0) + jnp.log(1.0 + jnp.exp2(-jnp.abs(x))) * LOG2E


def _layer_norm(x, g, b):
    mu = jnp.mean(x, -1, keepdims=True)
    xc = x - mu
    var = jnp.mean(xc * xc, -1, keepdims=True)
    return xc * lax.rsqrt(var + LN_EPS) * g + b


def _div_pow2(x, n):
    assert n & (n - 1) == 0
    return lax.shift_right_logical(x, n.bit_length() - 1)


def _mod_pow2(x, n):
    assert n & (n - 1) == 0
    return x & (n - 1)


def _params(*sem):
    return pltpu.CompilerParams(dimension_semantics=sem, vmem_limit_bytes=VMEM_LIMIT)


def _row_tile(m, want):
    t = min(m, want)
    assert m % t == 0
    return t


def _ln_kernel(x_ref, g_ref, b_ref, o_ref):
    o_ref[...] = _layer_norm(x_ref[...], g_ref[...], b_ref[...])


def _entry_ln(x2, g, b):
    m, d = x2.shape
    tm = _row_tile(m, 512)
    return pl.pallas_call(
        _ln_kernel,
        grid=(m // tm,),
        in_specs=[pl.BlockSpec((tm, d), lambda i: (i, 0)),
                  pl.BlockSpec((1, d), lambda i: (0, 0)),
                  pl.BlockSpec((1, d), lambda i: (0, 0))],
        out_specs=pl.BlockSpec((tm, d), lambda i: (i, 0)),
        out_shape=jax.ShapeDtypeStruct((m, d), F32),
        compiler_params=_params("parallel"),
        name="entry_ln",
    )(x2, g.reshape(1, d), b.reshape(1, d))


PROJ_WIDTHS = (3 * MIX_W, RWKV_IN, 4 * MIX_W, N_BRANCH * D_MODEL)
PROJ_DTYPES = (BF16, F32, F32, BF16)
PROJ_STEP = 256


def _proj_kernel(h_ref, w_ref, *out_refs):
    hb = h_ref[...].astype(BF16)
    off = 0
    for out_ref in out_refs:
        is_gate = out_ref is out_refs[-1]
        for n0 in range(0, out_ref.shape[1], PROJ_STEP):
            y = _dot(hb, w_ref[:, off + n0:off + n0 + PROJ_STEP])
            if is_gate:
                y = _sigmoid(y)
            out_ref[:, n0:n0 + PROJ_STEP] = y.astype(out_ref.dtype)
        off += out_ref.shape[1]


def _project(h, w_all, layer):
    m, d = h.shape
    tm = _row_tile(m, 512)
    return pl.pallas_call(
        _proj_kernel,
        grid=(m // tm,),
        in_specs=[pl.BlockSpec((tm, d), lambda i: (i, 0)),
                  pl.BlockSpec((None, d, w_all.shape[2]), lambda i: (layer, 0, 0),
                               pipeline_mode=pl.Buffered(1))],
        out_specs=[pl.BlockSpec((tm, n), lambda i: (i, 0)) for n in PROJ_WIDTHS],
        out_shape=[jax.ShapeDtypeStruct((m, n), dt) for n, dt in zip(PROJ_WIDTHS, PROJ_DTYPES)],
        compiler_params=_params("parallel"),
        name="in_proj",
    )(h, w_all)


def _sb_kernel(q_ref, k_ref, v_ref, o_ref):
    blk = SB_BLK
    nq = q_ref.shape[0] // blk
    qb0 = pl.program_id(2) * nq
    lane = lax.broadcasted_iota(jnp.int32, (1, LANES), 1)
    m0 = lane < HEAD_DIM
    row = _mod_pow2(lax.broadcasted_iota(jnp.int32, (2 * blk, blk), 0), blk)
    col = lax.broadcasted_iota(jnp.int32, (2 * blk, blk), 1)
    strict = col < row
    krow = lax.broadcasted_iota(jnp.int32, (blk, blk), 0)
    kcol = lax.broadcasted_iota(jnp.int32, (blk, blk), 1)
    later = (krow > kcol).astype(BF16)
    q = (q_ref[...].astype(F32) * (HEAD_DIM ** -0.5 * LOG2E)).astype(BF16)
    zq = jnp.zeros((blk, LANES), BF16)
    qs = [jnp.concatenate([jnp.where(m0, q[j * blk:(j + 1) * blk], zq),
                           jnp.where(m0, zq, q[j * blk:(j + 1) * blk])], axis=0) for j in range(nq)]

    def steps(d, nsub, runs, accs, diagonal):
        units = [(s, j) for s in range(nsub) for j in range(nq)]
        kb = {u: qb0 + u[1] - d - u[0] for u in units}
        start = {u: pl.multiple_of(jnp.maximum(kb[u], 0) * blk, blk) for u in units}
        z = {u: _dot_nt(qs[u[1]], k_ref[pl.ds(start[u], blk), :]) for u in units}
        sp, spm, within = {}, {}, {}
        for u in units:
            sp[u] = _softplus2(z[u])
            spm[u] = jnp.where(strict, sp[u], 0.0) if (diagonal and u[0] == 0) else sp[u]
            within[u] = _dot(spm[u].astype(BF16), later)
        new_runs, new_accs = [], []
        for j in range(nq):
            run = runs[j]
            ws, vs = [], []
            for s in range(nsub):
                u = (s, j)
                on_diagonal = diagonal and s == 0
                if not on_diagonal:
                    run = jnp.where(kb[u] >= 0, run, 1e30)
                w = jnp.exp2((z[u] - sp[u]) - (within[u] + run))
                if on_diagonal:
                    w = jnp.where(strict, w, 0.0)
                vblk = v_ref[pl.ds(start[u], blk), :]
                ws += [w[0:blk], w[blk:]]
                vs += [jnp.where(m0, vblk, zq), jnp.where(m0, zq, vblk)]
                run = run + jnp.sum(spm[u], axis=1, keepdims=True)
            new_accs.append(accs[j] + _dot(jnp.concatenate(ws, axis=1).astype(BF16),
                                           jnp.concatenate(vs, axis=0)))
            new_runs.append(run)
        return new_runs, new_accs

    runs, accs = steps(0, SB_HEAD_STEPS, [jnp.zeros((2 * blk, 1), F32)] * nq,
                       [jnp.zeros((blk, LANES), F32)] * nq, True)

    def alive_after(d, runs):
        alive = jnp.bool_(False)
        for j in range(nq):
            alive = jnp.logical_or(alive, jnp.logical_and(qb0 + j - d - 1 >= 0,
                                                          jnp.min(runs[j]) < SB_DEAD * LOG2E))
        return alive

    def cond(carry):
        return carry[1]

    def body(carry):
        d, _, runs, accs = carry
        new_runs, new_accs = steps(d, SB_SUBSTEPS, runs, accs, False)
        return d + SB_SUBSTEPS, alive_after(d + SB_SUBSTEPS - 1, new_runs), new_runs, new_accs

    init = (jnp.int32(SB_HEAD_STEPS), alive_after(SB_HEAD_STEPS - 1, runs), runs, accs)
    accs = lax.while_loop(cond, body, init)[3]
    for j in range(nq):
        o_ref[j * blk:(j + 1) * blk, :] = accs[j].astype(o_ref.dtype)


def _sb_attention(qkv, batch, seq):
    blk = _row_tile(seq, SB_TILE)
    ncol = MIX_W // LANES
    x3 = qkv.reshape(batch, seq, 3 * MIX_W)
    out = pl.pallas_call(
        _sb_kernel,
        grid=(batch, ncol, seq // blk),
        in_specs=[pl.BlockSpec((None, blk, LANES), lambda b, c, i: (b, i, c)),
                  pl.BlockSpec((None, seq, LANES), lambda b, c, i: (b, 0, ncol + c)),
                  pl.BlockSpec((None, seq, LANES), lambda b, c, i: (b, 0, 2 * ncol + c))],
        out_specs=pl.BlockSpec((None, blk, LANES), lambda b, c, i: (b, i, c)),
        out_shape=jax.ShapeDtypeStruct((batch, seq, MIX_W), BF16),
        compiler_params=_params("parallel", "parallel", "arbitrary"),
        name="stick_breaking",
    )(x3, x3, x3)
    return out.reshape(batch * seq, MIX_W)


def _head_sum_matrix(width, head):
    r = _div_pow2(lax.broadcasted_iota(jnp.int32, (width, width), 0), head)
    c = _div_pow2(lax.broadcasted_iota(jnp.int32, (width, width), 1), head)
    return (r == c).astype(BF16)


def _rwkv_kernel(p_ref, prev_ref, mu_ref, w0_ref, wup_ref, a0_ref, aup_ref, gup_ref,
                 kk_ref, ka_ref, rk_ref, gng_ref, gnb_ref, hsum_ref, cum_ref, o_ref,
                 state_ref, y_s, *prepared):
    nseq, tt = p_ref.shape[0], p_ref.shape[1]
    c = RW_CHUNK
    nch = tt // c
    w = MIX_W
    npair = w // LANES
    ti = pl.program_id(1)

    @pl.when(ti == 0)
    def _():
        state_ref[...] = jnp.zeros_like(state_ref)

    hsum = hsum_ref[...]

    def prepare(bi, kkd_s, rd_s, kinv_s, binv_s, kend_s, bend_s, v_s, g_s, bonus_s, decay_s):
        p = p_ref[bi]
        prev = jnp.where(ti == 0, 0.0, prev_ref[bi, 7:8, :])
        rowid = lax.broadcasted_iota(jnp.int32, (tt, 1), 0)
        shifted = jnp.where(rowid == 0, prev, pltpu.roll(p, 1, 0))
        xs = p + (shifted - p) * mu_ref[...]
        r = xs[:, 0:w]
        k = xs[:, w:2 * w]
        v = xs[:, 2 * w:3 * w]
        lora_in = xs[:, 3 * w:3 * w + LANES]
        gd = xs[:, 3 * w + LANES:3 * w + 2 * LANES]
        dec = w0_ref[...] + _dot(jnp.tanh(lora_in).astype(BF16), wup_ref[...])
        lw = -jnp.exp(-_softplus(-dec) - 0.5)
        a = _sigmoid(a0_ref[...] + _dot(lora_in.astype(BF16), aup_ref[...]))
        g_s[bi] = _dot(_sigmoid(gd).astype(BF16), gup_ref[...])

        def head_sum(x):
            return jnp.concatenate([_dot(x[:, i * LANES:(i + 1) * LANES].astype(BF16), hsum)
                                    for i in range(npair)], axis=1)

        kk = k * kk_ref[...]
        kkn = kk * lax.rsqrt(jnp.maximum(head_sum(kk * kk), 1e-24))
        k2 = k * (1.0 + (a - 1.0) * ka_ref[...])
        bvec = kkn * a
        bonus_s[bi] = head_sum(r * k2 * rk_ref[...]) * v
        lw_hi = lw.astype(BF16)
        lw_lo = (lw - lw_hi.astype(F32)).astype(BF16)
        cs = _dot(cum_ref[...], lw_hi) + _dot(cum_ref[...], lw_lo)
        tot = cs.reshape(nch, c, w)[:, c - 1:c, :]
        to_end = (tot - cs.reshape(nch, c, w)).reshape(tt, w)
        p_inv = jnp.exp(-cs)
        p_end = jnp.exp(to_end)
        kkd_s[bi] = (kkn * jnp.exp(cs - lw)).astype(BF16)
        rd_s[bi] = (r * jnp.exp(cs)).astype(BF16)
        kinv_s[bi] = (k2 * p_inv).astype(BF16)
        binv_s[bi] = (bvec * p_inv).astype(BF16)
        kend_s[bi] = (k2 * p_end).astype(BF16)
        bend_s[bi] = (-(bvec * p_end)).astype(BF16)
        v_s[bi] = v.astype(BF16)
        decay_s[bi] = jnp.exp(tot)

    _rwkv_scan_tile(lambda bi: prepare(bi, *prepared), prepared, state_ref, y_s, hsum, gng_ref, gnb_ref, o_ref)


def _rwkv_scan_tile(prepare_fn, scratch_set, state_ref, y_s, hsum, gng_ref, gnb_ref, o_ref):
    kkd_s, rd_s, kinv_s, binv_s, kend_s, bend_s, v_s, g_s, bonus_s, decay_s = scratch_set
    nseq, tt = y_s.shape[0], y_s.shape[1]
    c = RW_CHUNK
    c2 = 2 * c
    nch = tt // c
    npair = MIX_W // LANES

    def rd(ref, u):
        return ref[u[0], u[1] * c:(u[1] + 1) * c, u[2] * LANES:(u[2] + 1) * LANES]

    lane = lax.broadcasted_iota(jnp.int32, (1, LANES), 1)
    m0 = lane < HEAD_DIM
    row = lax.broadcasted_iota(jnp.int32, (c2, c2), 0)
    col = lax.broadcasted_iota(jnp.int32, (c2, c2), 1)
    same_head = _div_pow2(row, c) == _div_pow2(col, c)
    strict = jnp.logical_and(same_head, _mod_pow2(row, c) > _mod_pow2(col, c))
    incl = jnp.logical_and(same_head, _mod_pow2(row, c) >= _mod_pow2(col, c))
    eye = (row == col).astype(F32)
    head_block = same_head.astype(F32)
    zb = jnp.zeros((c, LANES), BF16)

    def stack(x):
        return jnp.concatenate([jnp.where(m0, x, zb), jnp.where(m0, zb, x)], axis=0)

    units = [(bi, ci, pi) for ci in range(nch) for bi in range(nseq) for pi in range(npair)]
    rows = {u: slice(u[1] * c, (u[1] + 1) * c) for u in units}
    lanes = {u: slice(u[2] * LANES, (u[2] + 1) * LANES) for u in units}
    chain = {u: u[0] * npair + u[2] for u in units}
    left, a_rb, from_v, inv = {}, {}, {}, {}
    for bi in range(nseq):
        prepare_fn(bi)
        ub = [u for u in units if u[0] == bi]
        scores, a_kb, power = {}, {}, {}
        for u in ub:
            left[u] = jnp.concatenate([stack(rd(kkd_s, u)), stack(rd(rd_s, u))], axis=0)
        for u in ub:
            scores[u] = _dot_nt(left[u], jnp.concatenate([stack(rd(kinv_s, u)), stack(rd(binv_s, u))], axis=0))
        for u in ub:
            a_kb[u] = jnp.where(strict, scores[u][0:c2, c2:2 * c2], 0.0)
            a_rb[u] = jnp.where(incl, scores[u][c2:2 * c2, c2:2 * c2], 0.0).astype(BF16)
        for u in ub:
            from_v[u] = _dot(jnp.concatenate([jnp.where(strict, scores[u][0:c2, 0:c2], 0.0),
                                              jnp.where(incl, scores[u][c2:2 * c2, 0:c2], 0.0)],
                                             axis=0).astype(BF16), stack(rd(v_s, u)))
        for u in ub:
            inv[u] = eye - a_kb[u]
            ab = a_kb[u].astype(BF16)
            power[u] = _dot(ab, ab).astype(BF16)
        for lvl in range(5):
            for u in ub:
                if lvl < 4:
                    both = _dot(jnp.concatenate([inv[u].astype(BF16), power[u]], axis=0), power[u])
                    inv[u] = inv[u] + both[0:c2]
                    power[u] = both[c2:].astype(BF16)
                else:
                    inv[u] = (inv[u] + _dot(inv[u].astype(BF16), power[u])).astype(BF16)
    states = [state_ref[i] for i in range(nseq * npair)]
    for ci in range(nch):
        cu = [u for u in units if u[1] == ci]
        from_state = {u: _dot_nt(left[u], states[chain[u]].astype(BF16)) for u in cu}
        u_stack = {u: _dot(inv[u], (from_state[u][0:c2] + from_v[u][0:c2]).astype(BF16)) for u in cu}
        y_stack = {u: from_state[u][c2:] + from_v[u][c2:] - _dot(a_rb[u], u_stack[u].astype(BF16)) for u in cu}
        for u in cu:
            uu = u_stack[u][0:c] + u_stack[u][c:]
            vu = jnp.concatenate([rd(v_s, u), uu.astype(BF16)], axis=0)
            kb_end = jnp.concatenate([rd(kend_s, u), rd(bend_s, u)], axis=0)
            states[chain[u]] = (states[chain[u]] * decay_s[u[0], ci, :, lanes[u]]
                                + head_block * _dot_tn(vu, kb_end))
            y_s[u[0], rows[u], lanes[u]] = y_stack[u][0:c] + y_stack[u][c:]
    for i in range(nseq * npair):
        state_ref[i] = states[i]

    for bi in range(nseq):
        for pi in range(npair):
            ls = slice(pi * LANES, (pi + 1) * LANES)
            y = y_s[bi, :, ls]
            mean = _dot(y.astype(BF16), hsum) * (1.0 / HEAD_DIM)
            yc = y - mean
            var = _dot((yc * yc).astype(BF16), hsum) * (1.0 / HEAD_DIM)
            yn = yc * lax.rsqrt(var + RWKV_GN_EPS) * gng_ref[:, ls] + gnb_ref[:, ls]
            o_ref[bi, :, ls] = ((yn + bonus_s[bi, :, ls]) * g_s[bi, :, ls]).astype(o_ref.dtype)


def _rwkv(p_rw, batch, seq, mu, w0, w_up_pad, a0, a_up_pad, g_up, k_k, k_a, r_k, gn_g, gn_b):
    n = p_rw.shape[1]
    w = MIX_W
    tt = _row_tile(seq, RW_TILE)
    nt = seq // tt
    vec = lambda width: pl.BlockSpec((1, width), lambda b, t: (0, 0))
    mat = pl.BlockSpec((LANES, w), lambda b, t: (0, 0))
    ns = RW_SEQS if batch % RW_SEQS == 0 else 1
    tile_bf16 = pltpu.VMEM((ns, tt, w), BF16)
    tile_f32 = pltpu.VMEM((ns, tt, w), F32)
    scratch_set = [tile_bf16] * 7 + [tile_f32] * 2 + [pltpu.VMEM((ns, tt // RW_CHUNK, 1, w), F32)]
    p3 = p_rw.reshape(batch, seq, n)
    hsum = _head_sum_matrix(LANES, HEAD_DIM)
    trow = lax.broadcasted_iota(jnp.int32, (tt, tt), 0)
    tcol = lax.broadcasted_iota(jnp.int32, (tt, tt), 1)
    cum = jnp.logical_and(trow // RW_CHUNK == tcol // RW_CHUNK, trow >= tcol).astype(BF16)
    whole = lambda a: pl.BlockSpec(a.shape, lambda b, t: (0, 0))
    out = pl.pallas_call(
        _rwkv_kernel,
        grid=(batch // ns, nt),
        in_specs=[pl.BlockSpec((ns, tt, n), lambda b, t: (b, t, 0)),
                  pl.BlockSpec((ns, 8, n), lambda b, t: (b, jnp.maximum(t * (tt // 8) - 1, 0), 0)),
                  vec(n), vec(w), mat, vec(w), mat, mat, vec(w), vec(w), vec(w), vec(w), vec(w),
                  whole(hsum), whole(cum)],
        out_specs=pl.BlockSpec((ns, tt, w), lambda b, t: (b, t, 0)),
        out_shape=jax.ShapeDtypeStruct((batch, seq, w), BF16),
        scratch_shapes=[pltpu.VMEM((ns * (w // LANES), LANES, LANES), F32), tile_f32] + scratch_set,
        compiler_params=_params("parallel", "arbitrary"),
        name="rwkv7",
    )(p3, p3, mu.reshape(1, n), w0.reshape(1, w), w_up_pad, a0.reshape(1, w), a_up_pad, g_up,
      k_k.reshape(1, w), k_a.reshape(1, w), r_k.reshape(1, w), gn_g.reshape(1, w), gn_b.reshape(1, w),
      hsum, cum)
    return out.reshape(batch * seq, w)


def _hgrn_kernel(q_ref, f_ref, i_ref, g_ref, lbl_ref, ng_ref, o_ref, state_ref, *, layer):
    hc = HG_CHUNK
    tile = min(HG_TILE, q_ref.shape[0])
    nch = tile // hc
    seq = q_ref.shape[0]
    logits = lbl_ref[...]
    e = jnp.exp(logits - jnp.max(logits, axis=0, keepdims=True))
    probs = e / jnp.sum(e, axis=0, keepdims=True)
    lb = jnp.zeros((1, LANES), F32)
    for j in range(1, layer + 1):
        lb = lb + probs[j:j + 1, :]
    ng = ng_ref[...]
    pos = _mod_pow2(lax.broadcasted_iota(jnp.int32, (tile, 1), 0), hc)
    assert hc == 2 * SUBLANES
    pos8 = lax.broadcasted_iota(jnp.int32, (nch, SUBLANES, 1), 1)
    half_rows = nch * SUBLANES
    same_chunk = (_div_pow2(lax.broadcasted_iota(jnp.int32, (half_rows, half_rows), 0), SUBLANES)
                  == _div_pow2(lax.broadcasted_iota(jnp.int32, (half_rows, half_rows), 1), SUBLANES))
    state_ref[...] = jnp.zeros_like(state_ref)

    def tile_body(ti, _):
        rows = pl.ds(pl.multiple_of(ti * tile, tile), tile)
        q = q_ref[rows, :]
        fgate = lb + (1.0 - lb) * _sigmoid(f_ref[rows, :])
        xi = i_ref[rows, :]
        v = xi * _sigmoid(xi)
        k = 1.0 - fgate
        b = jnp.log(fgate)
        for d in (1, 2, 4, 8):
            b = b + jnp.where(pos >= d, pltpu.roll(b, d, 0), 0.0)
        b3 = b.reshape(nch, hc, LANES)
        q3 = q.reshape(nch, hc, LANES)
        k3 = k.reshape(nch, hc, LANES)
        v3 = v.reshape(nch, hc, LANES)
        tgt = b3 * LOG2E
        src = (b3 - jnp.log(k3)) * LOG2E
        lo, hi = slice(0, SUBLANES), slice(SUBLANES, hc)
        o_half = [jnp.zeros((nch, SUBLANES, LANES), F32) for _ in range(2)]
        for s in range(hc):
            h = s // SUBLANES
            rows_h = (lo, hi)[h]
            arg = tgt[:, rows_h, :] - src[:, s:s + 1, :]
            if s % SUBLANES:
                arg = jnp.where(pos8 >= s % SUBLANES, arg, -1e30)
            wgt = jnp.sum(q3[:, rows_h, :] * jnp.exp2(arg), axis=-1, keepdims=True)
            o_half[h] = o_half[h] + wgt * v3[:, s:s + 1, :]
        b_last = b3[:, hc - 1:hc, :]
        qd = (q3 * jnp.exp(b3)).astype(BF16)
        k_end = (k3 * jnp.exp(b_last - b3)).astype(BF16)
        carry = jnp.exp(b_last)
        vb = v3.astype(BF16)
        updates = [_dot_tn(vb[ch], k_end[ch]) for ch in range(nch)]
        states = [state_ref[...]]
        for ch in range(nch):
            states.append(states[ch] * carry[ch] + updates[ch])
        state_ref[...] = states[nch]
        inter = [_dot_nt(qd[ch], states[ch].astype(BF16)) for ch in range(nch)]
        ref = tgt[:, SUBLANES - 1:SUBLANES, :]
        q_hi = (q3[:, hi, :] * jnp.exp2(tgt[:, hi, :] - ref)).reshape(nch * SUBLANES, LANES).astype(BF16)
        k_lo = jnp.exp2(ref - src[:, lo, :]).reshape(nch * SUBLANES, LANES).astype(BF16)
        cross = jnp.where(same_chunk, _dot_nt(q_hi, k_lo), 0.0).astype(BF16)
        v_lo = v3[:, lo, :].reshape(nch * SUBLANES, LANES).astype(BF16)
        o_cross = _dot(cross, v_lo).reshape(nch, SUBLANES, LANES)
        o3 = jnp.concatenate([o_half[0], o_half[1] + o_cross], axis=1)
        o = o3.reshape(tile, LANES) + jnp.concatenate(inter, axis=0)
        o = o * lax.rsqrt(jnp.mean(o * o, axis=-1, keepdims=True) + RMS_EPS)
        gt = g_ref[rows, :]
        o_ref[rows, :] = (o * ng * (gt * _sigmoid(gt))).astype(o_ref.dtype)
        return 0

    lax.fori_loop(0, seq // tile, tile_body, 0)


def _hgrn(p_hg, batch, seq, lb_logits, norm_g, layer):
    nh = HGRN_HEADS
    x3 = p_hg.reshape(batch, seq, 4 * MIX_W)
    col = lambda j: pl.BlockSpec((None, seq, LANES), lambda b, h: (b, 0, j * nh + h))
    out = pl.pallas_call(
        functools.partial(_hgrn_kernel, layer=layer),
        grid=(batch, nh),
        in_specs=[col(0), col(1), col(2), col(3),
                  pl.BlockSpec((DEPTH, LANES), lambda b, h: (0, h)),
                  pl.BlockSpec((1, LANES), lambda b, h: (0, h))],
        out_specs=pl.BlockSpec((None, seq, LANES), lambda b, h: (b, 0, h)),
        out_shape=jax.ShapeDtypeStruct((batch, seq, MIX_W), BF16),
        scratch_shapes=[pltpu.VMEM((LANES, LANES), F32)],
        compiler_params=_params("parallel", "parallel"),
        name="hgrn2",
    )(x3, x3, x3, x3, lb_logits, norm_g.reshape(1, MIX_W))
    return out.reshape(batch * seq, MIX_W)


def _merge_kernel(h_ref, ysb_ref, yrw_ref, yhg_ref, gate_ref, wup_ref, wout_ref, g_ref, b_ref, o_ref):
    d = D_MODEL
    tm = h_ref.shape[0]
    sub = min(tm, DENSE_SUBTILE)
    for r0 in range(0, tm, sub):
        rows = slice(r0, r0 + sub)
        merged = None
        for j, y_ref in enumerate((ysb_ref, yrw_ref, yhg_ref)):
            term = gate_ref[rows, j * d:(j + 1) * d].astype(F32) * _dot(y_ref[rows, :], wup_ref[j])
            merged = term if merged is None else merged + term
        x = ALPHA * h_ref[rows, :] + _dot(merged.astype(BF16), wout_ref[...])
        o_ref[rows, :] = _layer_norm(x, g_ref[...], b_ref[...])


def _merge(h, y_sb, y_rw, y_hg, gates, w_up, w_out, g, b):
    m, d = h.shape
    tm = _row_tile(m, 512)
    rows = lambda width: pl.BlockSpec((tm, width), lambda i: (i, 0))
    return pl.pallas_call(
        _merge_kernel,
        grid=(m // tm,),
        in_specs=[rows(d), rows(MIX_W), rows(MIX_W), rows(MIX_W), rows(N_BRANCH * d),
                  pl.BlockSpec((N_BRANCH, MIX_W, d), lambda i: (0, 0, 0)),
                  pl.BlockSpec((d, d), lambda i: (0, 0)),
                  pl.BlockSpec((1, d), lambda i: (0, 0)),
                  pl.BlockSpec((1, d), lambda i: (0, 0))],
        out_specs=rows(d),
        out_shape=jax.ShapeDtypeStruct((m, d), F32),
        compiler_params=_params("parallel"),
        name="merge_out",
    )(h, y_sb, y_rw, y_hg, gates, w_up, w_out, g.reshape(1, d), b.reshape(1, d))


def _ffn_kernel(h_ref, wa_ref, wb_ref, wo_ref, g_ref, b_ref, o_ref):
    tm = h_ref.shape[0]
    sub = min(tm, DENSE_SUBTILE)
    for r0 in range(0, tm, sub):
        h = h_ref[r0:r0 + sub, :]
        hb = h.astype(BF16)
        a = _dot(hb, wa_ref[...])
        act = (a * _sigmoid(a) * _dot(hb, wb_ref[...])).astype(BF16)
        o_ref[r0:r0 + sub, :] = _layer_norm(ALPHA * h + _dot(act, wo_ref[...]), g_ref[...], b_ref[...])


def _ffn(h, w_in, w_out, g, b):
    m, d = h.shape
    tm = _row_tile(m, 512)
    return pl.pallas_call(
        _ffn_kernel,
        grid=(m // tm,),
        in_specs=[pl.BlockSpec((tm, d), lambda i: (i, 0)),
                  pl.BlockSpec((d, D_FF), lambda i: (0, 0)),
                  pl.BlockSpec((d, D_FF), lambda i: (0, 1)),
                  pl.BlockSpec((D_FF, d), lambda i: (0, 0)),
                  pl.BlockSpec((1, d), lambda i: (0, 0)),
                  pl.BlockSpec((1, d), lambda i: (0, 0))],
        out_specs=pl.BlockSpec((tm, d), lambda i: (i, 0)),
        out_shape=jax.ShapeDtypeStruct((m, d), F32),
        compiler_params=_params("parallel"),
        name="ffn",
    )(h, w_in, w_in, w_out, g.reshape(1, d), b.reshape(1, d))


def kernel(x, ln_in_g, ln_in_b, w_in, rwkv_mu, rwkv_w0, rwkv_w_up, rwkv_a0, rwkv_a_up, rwkv_g_up,
           rwkv_k_k, rwkv_k_a, rwkv_r_k, rwkv_ln_g, rwkv_ln_b, hgrn_lb_logits, hgrn_norm_g,
           w_branch_up, w_out, ln1_g, ln1_b, w_ffn_in, w_ffn_out, ln2_g, ln2_b):
    batch, seq, d = x.shape
    m = batch * seq
    w = MIX_W
    assert w_in.shape[-1] == sum(PROJ_WIDTHS)
    w_in_b = w_in.astype(BF16)
    zeros_lora = jnp.zeros((DEPTH, DECAY_LORA, w), BF16)
    w_up_pad = jnp.concatenate([rwkv_w_up.astype(BF16), zeros_lora], axis=1)
    a_up_pad = jnp.concatenate([zeros_lora, rwkv_a_up.astype(BF16)], axis=1)
    g_up_b = rwkv_g_up.astype(BF16)
    w_branch_b = w_branch_up.astype(BF16)
    w_out_b = w_out.astype(BF16)
    w_ffn_in_b = w_ffn_in.astype(BF16)
    w_ffn_out_b = w_ffn_out.astype(BF16)
    lb_logits = hgrn_lb_logits.astype(F32)

    h = _entry_ln(x.reshape(m, d), ln_in_g, ln_in_b)
    for l in range(DEPTH):
        qkv, p_rw, p_hg, gates = _project(h, w_in_b, l)
        y_sb = _sb_attention(qkv, batch, seq)
        y_rw = _rwkv(p_rw, batch, seq, rwkv_mu[l], rwkv_w0[l], w_up_pad[l], rwkv_a0[l], a_up_pad[l],
                     g_up_b[l], rwkv_k_k[l], rwkv_k_a[l], rwkv_r_k[l], rwkv_ln_g[l], rwkv_ln_b[l])
        y_hg = _hgrn(p_hg, batch, seq, lb_logits, hgrn_norm_g[l], l)
        h = _merge(h, y_sb, y_rw, y_hg, gates, w_branch_b[l], w_out_b[l], ln1_g[l], ln1_b[l])
        h = _ffn(h, w_ffn_in_b[l], w_ffn_out_b[l], ln2_g[l], ln2_b[l])
    return h.reshape(batch, seq, d).astype(x.dtype)
```

```python
import functools

import jax
import jax.numpy as jnp
from jax import lax
from jax.experimental import pallas as pl
from jax.experimental.pallas import tpu as pltpu

F32 = jnp.float32
BF16 = jnp.bfloat16

D_MODEL = 1024
DEPTH = 4
MIX_W = D_MODEL // 2
HEAD_DIM = 64
DECAY_LORA = 64
ICL_LORA = 64
GATE_LORA = 128
RWKV_IN = 3 * MIX_W + DECAY_LORA + ICL_LORA + GATE_LORA
HGRN_EXPAND = 128
HGRN_HEADS = MIX_W // HGRN_EXPAND
N_BRANCH = 3
D_FF = -(-(8 * D_MODEL) // (3 * 256)) * 256
ALPHA = (2 * DEPTH) ** 0.25
LN_EPS = 1e-5
RWKV_GN_EPS = 64e-5
RMS_EPS = 1e-6

LANES = 128
SUBLANES = 8
LOG2E = 1.4426950408889634
SB_BLK = 128
SB_TILE = 512
SB_HEAD_STEPS = 3
SB_SUBSTEPS = 2
SB_DEAD = 104.0
RW_CHUNK = 64
RW_TILE = 256
RW_SEQS = 4
HG_CHUNK = 16
HG_TILE = 1024
DENSE_SUBTILE = 128
VMEM_LIMIT = 56 * 1024 * 1024

_NT = (((1,), (1,)), ((), ()))
_TN = (((0,), (0,)), ((), ()))


def _dot(a, b):
    return jnp.dot(a, b, preferred_element_type=F32)


def _dot_nt(a, b):
    return lax.dot_general(a, b, _NT, preferred_element_type=F32)


def _dot_tn(a, b):
    return lax.dot_general(a, b, _TN, preferred_element_type=F32)


def _dot2(x, w):
    hi = x.astype(BF16)
    lo = (x - hi.astype(F32)).astype(BF16)
    return _dot(hi, w) + _dot(lo, w)


def _sigmoid(x):
    return 1.0 / (1.0 + jnp.exp(-x))


def _softplus(x):
    return jnp.maximum(x, 0.0) + jnp.log(1.0 + jnp.exp(-jnp.abs(x)))


def _softplus2(x):
    return jnp.maximum(x, 0.0) + jnp.log(1.0 + jnp.exp2(-jnp.abs(x))) * LOG2E


def _layer_norm(x, g, b):
    mu = jnp.mean(x, -1, keepdims=True)
    xc = x - mu
    var = jnp.mean(xc * xc, -1, keepdims=True)
    return xc * lax.rsqrt(var + LN_EPS) * g + b


def _div_pow2(x, n):
    assert n & (n - 1) == 0
    return lax.shift_right_logical(x, n.bit_length() - 1)


def _mod_pow2(x, n):
    assert n & (n - 1) == 0
    return x & (n - 1)


def _params(*sem):
    return pltpu.CompilerParams(dimension_semantics=sem, vmem_limit_bytes=VMEM_LIMIT)


def _row_tile(m, want):
    t = min(m, want)
    assert m % t == 0
    return t


def _ln_kernel(x_ref, g_ref, b_ref, o_ref):
    o_ref[...] = _layer_norm(x_ref[...], g_ref[...], b_ref[...])


def _entry_ln(x2, g, b):
    m, d = x2.shape
    tm = _row_tile(m, 512)
    return pl.pallas_call(
        _ln_kernel,
        grid=(m // tm,),
        in_specs=[pl.BlockSpec((tm, d), lambda i: (i, 0)),
                  pl.BlockSpec((1, d), lambda i: (0, 0)),
                  pl.BlockSpec((1, d), lambda i: (0, 0))],
        out_specs=pl.BlockSpec((tm, d), lambda i: (i, 0)),
        out_shape=jax.ShapeDtypeStruct((m, d), F32),
        compiler_params=_params("parallel"),
        name="entry_ln",
    )(x2, g.reshape(1, d), b.reshape(1, d))


PROJ_WIDTHS = (3 * MIX_W, RWKV_IN, 4 * MIX_W, N_BRANCH * D_MODEL)
PROJ_DTYPES = (BF16, F32, F32, BF16)
PROJ_STEP = 256


def _proj_kernel(h_ref, w_ref, *out_refs):
    hb = h_ref[...].astype(BF16)
    off = 0
    for out_ref in out_refs:
        is_gate = out_ref is out_refs[-1]
        for n0 in range(0, out_ref.shape[1], PROJ_STEP):
            y = _dot(hb, w_ref[:, off + n0:off + n0 + PROJ_STEP])
            if is_gate:
                y = _sigmoid(y)
            out_ref[:, n0:n0 + PROJ_STEP] = y.astype(out_ref.dtype)
        off += out_ref.shape[1]


def _project(h, w_all, layer):
    m, d = h.shape
    tm = _row_tile(m, 512)
    return pl.pallas_call(
        _proj_kernel,
        grid=(m // tm,),
        in_specs=[pl.BlockSpec((tm, d), lambda i: (i, 0)),
                  pl.BlockSpec((None, d, w_all.shape[2]), lambda i: (layer, 0, 0),
                               pipeline_mode=pl.Buffered(1))],
        out_specs=[pl.BlockSpec((tm, n), lambda i: (i, 0)) for n in PROJ_WIDTHS],
        out_shape=[jax.ShapeDtypeStruct((m, n), dt) for n, dt in zip(PROJ_WIDTHS, PROJ_DTYPES)],
        compiler_params=_params("parallel"),
        name="in_proj",
    )(h, w_all)


def _sb_kernel(q_ref, k_ref, v_ref, o_ref):
    blk = SB_BLK
    nq = q_ref.shape[0] // blk
    qb0 = pl.program_id(2) * nq
    lane = lax.broadcasted_iota(jnp.int32, (1, LANES), 1)
    m0 = lane < HEAD_DIM
    row = _mod_pow2(lax.broadcasted_iota(jnp.int32, (2 * blk, blk), 0), blk)
    col = lax.broadcasted_iota(jnp.int32, (2 * blk, blk), 1)
    strict = col < row
    krow = lax.broadcasted_iota(jnp.int32, (blk, blk), 0)
    kcol = lax.broadcasted_iota(jnp.int32, (blk, blk), 1)
    later = (krow > kcol).astype(BF16)
    q = (q_ref[...].astype(F32) * (HEAD_DIM ** -0.5 * LOG2E)).astype(BF16)
    zq = jnp.zeros((blk, LANES), BF16)
    qs = [jnp.concatenate([jnp.where(m0, q[j * blk:(j + 1) * blk], zq),
                           jnp.where(m0, zq, q[j * blk:(j + 1) * blk])], axis=0) for j in range(nq)]

    def steps(d, nsub, runs, accs, diagonal):
        units = [(s, j) for s in range(nsub) for j in range(nq)]
        kb = {u: qb0 + u[1] - d - u[0] for u in units}
        start = {u: pl.multiple_of(jnp.maximum(kb[u], 0) * blk, blk) for u in units}
        z = {u: _dot_nt(qs[u[1]], k_ref[pl.ds(start[u], blk), :]) for u in units}
        sp, spm, within = {}, {}, {}
        for u in units:
            sp[u] = _softplus2(z[u])
            spm[u] = jnp.where(strict, sp[u], 0.0) if (diagonal and u[0] == 0) else sp[u]
            within[u] = _dot(spm[u].astype(BF16), later)
        new_runs, new_accs = [], []
        for j in range(nq):
            run = runs[j]
            ws, vs = [], []
            for s in range(nsub):
                u = (s, j)
                on_diagonal = diagonal and s == 0
                if not on_diagonal:
                    run = jnp.where(kb[u] >= 0, run, 1e30)
                w = jnp.exp2((z[u] - sp[u]) - (within[u] + run))
                if on_diagonal:
                    w = jnp.where(strict, w, 0.0)
                vblk = v_ref[pl.ds(start[u], blk), :]
                ws += [w[0:blk], w[blk:]]
                vs += [jnp.where(m0, vblk, zq), jnp.where(m0, zq, vblk)]
                run = run + jnp.sum(spm[u], axis=1, keepdims=True)
            new_accs.append(accs[j] + _dot(jnp.concatenate(ws, axis=1).astype(BF16),
                                           jnp.concatenate(vs, axis=0)))
            new_runs.append(run)
        return new_runs, new_accs

    runs, accs = steps(0, SB_HEAD_STEPS, [jnp.zeros((2 * blk, 1), F32)] * nq,
                       [jnp.zeros((blk, LANES), F32)] * nq, True)

    def alive_after(d, runs):
        alive = jnp.bool_(False)
        for j in range(nq):
            alive = jnp.logical_or(alive, jnp.logical_and(qb0 + j - d - 1 >= 0,
                                                          jnp.min(runs[j]) < SB_DEAD * LOG2E))
        return alive

    def cond(carry):
        return carry[1]

    def body(carry):
        d, _, runs, accs = carry
        new_runs, new_accs = steps(d, SB_SUBSTEPS, runs, accs, False)
        return d + SB_SUBSTEPS, alive_after(d + SB_SUBSTEPS - 1, new_runs), new_runs, new_accs

    init = (jnp.int32(SB_HEAD_STEPS), alive_after(SB_HEAD_STEPS - 1, runs), runs, accs)
    accs = lax.while_loop(cond, body, init)[3]
    for j in range(nq):
        o_ref[j * blk:(j + 1) * blk, :] = accs[j].astype(o_ref.dtype)


def _sb_attention(qkv, batch, seq):
    blk = _row_tile(seq, SB_TILE)
    ncol = MIX_W // LANES
    x3 = qkv.reshape(batch, seq, 3 * MIX_W)
    out = pl.pallas_call(
        _sb_kernel,
        grid=(batch, ncol, seq // blk),
        in_specs=[pl.BlockSpec((None, blk, LANES), lambda b, c, i: (b, i, c)),
                  pl.BlockSpec((None, seq, LANES), lambda b, c, i: (b, 0, ncol + c)),
                  pl.BlockSpec((None, seq, LANES), lambda b, c, i: (b, 0, 2 * ncol + c))],
        out_specs=pl.BlockSpec((None, blk, LANES), lambda b, c, i: (b, i, c)),
        out_shape=jax.ShapeDtypeStruct((batch, seq, MIX_W), BF16),
        compiler_params=_params("parallel", "parallel", "arbitrary"),
        name="stick_breaking",
    )(x3, x3, x3)
    return out.reshape(batch * seq, MIX_W)


def _head_sum_matrix(width, head):
    r = _div_pow2(lax.broadcasted_iota(jnp.int32, (width, width), 0), head)
    c = _div_pow2(lax.broadcasted_iota(jnp.int32, (width, width), 1), head)
    return (r == c).astype(BF16)


def _rwkv_kernel(p_ref, prev_ref, mu_ref, w0_ref, wup_ref, a0_ref, aup_ref, gup_ref,
                 kk_ref, ka_ref, rk_ref, gng_ref, gnb_ref, hsum_ref, cum_ref, o_ref,
                 state_ref, y_s, *prepared):
    nseq, tt = p_ref.shape[0], p_ref.shape[1]
    c = RW_CHUNK
    nch = tt // c
    w = MIX_W
    npair = w // LANES
    ti = pl.program_id(1)

    @pl.when(ti == 0)
    def _():
        state_ref[...] = jnp.zeros_like(state_ref)

    hsum = hsum_ref[...]

    def prepare(bi, kkd_s, rd_s, kinv_s, binv_s, kend_s, bend_s, v_s, g_s, bonus_s, decay_s):
        p = p_ref[bi]
        prev = jnp.where(ti == 0, 0.0, prev_ref[bi, 7:8, :])
        rowid = lax.broadcasted_iota(jnp.int32, (tt, 1), 0)
        shifted = jnp.where(rowid == 0, prev, pltpu.roll(p, 1, 0))
        xs = p + (shifted - p) * mu_ref[...]
        r = xs[:, 0:w]
        k = xs[:, w:2 * w]
        v = xs[:, 2 * w:3 * w]
        lora_in = xs[:, 3 * w:3 * w + LANES]
        gd = xs[:, 3 * w + LANES:3 * w + 2 * LANES]
        dec = w0_ref[...] + _dot(jnp.tanh(lora_in).astype(BF16), wup_ref[...])
        lw = -jnp.exp(-_softplus(-dec) - 0.5)
        a = _sigmoid(a0_ref[...] + _dot(lora_in.astype(BF16), aup_ref[...]))
        g_s[bi] = _dot(_sigmoid(gd).astype(BF16), gup_ref[...])

        def head_sum(x):
            return jnp.concatenate([_dot(x[:, i * LANES:(i + 1) * LANES].astype(BF16), hsum)
                                    for i in range(npair)], axis=1)

        kk = k * kk_ref[...]
        kkn = kk * lax.rsqrt(jnp.maximum(head_sum(kk * kk), 1e-24))
        k2 = k * (1.0 + (a - 1.0) * ka_ref[...])
        bvec = kkn * a
        bonus_s[bi] = head_sum(r * k2 * rk_ref[...]) * v
        lw_hi = lw.astype(BF16)
        lw_lo = (lw - lw_hi.astype(F32)).astype(BF16)
        cs = _dot(cum_ref[...], lw_hi) + _dot(cum_ref[...], lw_lo)
        tot = cs.reshape(nch, c, w)[:, c - 1:c, :]
        to_end = (tot - cs.reshape(nch, c, w)).reshape(tt, w)
        p_inv = jnp.exp(-cs)
        p_end = jnp.exp(to_end)
        kkd_s[bi] = (kkn * jnp.exp(cs - lw)).astype(BF16)
        rd_s[bi] = (r * jnp.exp(cs)).astype(BF16)
        kinv_s[bi] = (k2 * p_inv).astype(BF16)
        binv_s[bi] = (bvec * p_inv).astype(BF16)
        kend_s[bi] = (k2 * p_end).astype(BF16)
        bend_s[bi] = (-(bvec * p_end)).astype(BF16)
        v_s[bi] = v.astype(BF16)
        decay_s[bi] = jnp.exp(tot)

    _rwkv_scan_tile(lambda bi: prepare(bi, *prepared), prepared, state_ref, y_s, hsum, gng_ref, gnb_ref, o_ref)


def _rwkv_scan_tile(prepare_fn, scratch_set, state_ref, y_s, hsum, gng_ref, gnb_ref, o_ref):
    kkd_s, rd_s, kinv_s, binv_s, kend_s, bend_s, v_s, g_s, bonus_s, decay_s = scratch_set
    nseq, tt = y_s.shape[0], y_s.shape[1]
    c = RW_CHUNK
    c2 = 2 * c
    nch = tt // c
    npair = MIX_W // LANES

    def rd(ref, u):
        return ref[u[0], u[1] * c:(u[1] + 1) * c, u[2] * LANES:(u[2] + 1) * LANES]

    lane = lax.broadcasted_iota(jnp.int32, (1, LANES), 1)
    m0 = lane < HEAD_DIM
    row = lax.broadcasted_iota(jnp.int32, (c2, c2), 0)
    col = lax.broadcasted_iota(jnp.int32, (c2, c2), 1)
    same_head = _div_pow2(row, c) == _div_pow2(col, c)
    strict = jnp.logical_and(same_head, _mod_pow2(row, c) > _mod_pow2(col, c))
    incl = jnp.logical_and(same_head, _mod_pow2(row, c) >= _mod_pow2(col, c))
    eye = (row == col).astype(F32)
    head_block = same_head.astype(F32)
    zb = jnp.zeros((c, LANES), BF16)

    def stack(x):
        return jnp.concatenate([jnp.where(m0, x, zb), jnp.where(m0, zb, x)], axis=0)

    units = [(bi, ci, pi) for ci in range(nch) for bi in range(nseq) for pi in range(npair)]
    rows = {u: slice(u[1] * c, (u[1] + 1) * c) for u in units}
    lanes = {u: slice(u[2] * LANES, (u[2] + 1) * LANES) for u in units}
    chain = {u: u[0] * npair + u[2] for u in units}
    left, a_rb, from_v, inv = {}, {}, {}, {}
    for bi in range(nseq):
        prepare_fn(bi)
        ub = [u for u in units if u[0] == bi]
        scores, a_kb, power = {}, {}, {}
        for u in ub:
            left[u] = jnp.concatenate([stack(rd(kkd_s, u)), stack(rd(rd_s, u))], axis=0)
        for u in ub:
            scores[u] = _dot_nt(left[u], jnp.concatenate([stack(rd(kinv_s, u)), stack(rd(binv_s, u))], axis=0))
        for u in ub:
            a_kb[u] = jnp.where(strict, scores[u][0:c2, c2:2 * c2], 0.0)
            a_rb[u] = jnp.where(incl, scores[u][c2:2 * c2, c2:2 * c2], 0.0).astype(BF16)
        for u in ub:
            from_v[u] = _dot(jnp.concatenate([jnp.where(strict, scores[u][0:c2, 0:c2], 0.0),
                                              jnp.where(incl, scores[u][c2:2 * c2, 0:c2], 0.0)],
                                             axis=0).astype(BF16), stack(rd(v_s, u)))
        for u in ub:
            inv[u] = eye - a_kb[u]
            ab = a_kb[u].astype(BF16)
            power[u] = _dot(ab, ab).astype(BF16)
        for lvl in range(5):
            for u in ub:
                if lvl < 4:
                    both = _dot(jnp.concatenate([inv[u].astype(BF16), power[u]], axis=0), power[u])
                    inv[u] = inv[u] + both[0:c2]
                    power[u] = both[c2:].astype(BF16)
                else:
                    inv[u] = (inv[u] + _dot(inv[u].astype(BF16), power[u])).astype(BF16)
    states = [state_ref[i] for i in range(nseq * npair)]
    for ci in range(nch):
        cu = [u for u in units if u[1] == ci]
        from_state = {u: _dot_nt(left[u], states[chain[u]].astype(BF16)) for u in cu}
        u_stack = {u: _dot(inv[u], (from_state[u][0:c2] + from_v[u][0:c2]).astype(BF16)) for u in cu}
        y_stack = {u: from_state[u][c2:] + from_v[u][c2:] - _dot(a_rb[u], u_stack[u].astype(BF16)) for u in cu}
        for u in cu:
            uu = u_stack[u][0:c] + u_stack[u][c:]
            vu = jnp.concatenate([rd(v_s, u), uu.astype(BF16)], axis=0)
            kb_end = jnp.concatenate([rd(kend_s, u), rd(bend_s, u)], axis=0)
            states[chain[u]] = (states[chain[u]] * decay_s[u[0], ci, :, lanes[u]]
                                + head_block * _dot_tn(vu, kb_end))
            y_s[u[0], rows[u], lanes[u]] = y_stack[u][0:c] + y_stack[u][c:]
    for i in range(nseq * npair):
        state_ref[i] = states[i]

    for bi in range(nseq):
        for pi in range(npair):
            ls = slice(pi * LANES, (pi + 1) * LANES)
            y = y_s[bi, :, ls]
            mean = _dot(y.astype(BF16), hsum) * (1.0 / HEAD_DIM)
            yc = y - mean
            var = _dot((yc * yc).astype(BF16), hsum) * (1.0 / HEAD_DIM)
            yn = yc * lax.rsqrt(var + RWKV_GN_EPS) * gng_ref[:, ls] + gnb_ref[:, ls]
            o_ref[bi, :, ls] = ((yn + bonus_s[bi, :, ls]) * g_s[bi, :, ls]).astype(o_ref.dtype)


def _rwkv(p_rw, batch, seq, mu, w0, w_up_pad, a0, a_up_pad, g_up, k_k, k_a, r_k, gn_g, gn_b):
    n = p_rw.shape[1]
    w = MIX_W
    tt = _row_tile(seq, RW_TILE)
    nt = seq // tt
    vec = lambda width: pl.BlockSpec((1, width), lambda b, t: (0, 0))
    mat = pl.BlockSpec((LANES, w), lambda b, t: (0, 0))
    ns = RW_SEQS if batch % RW_SEQS == 0 else 1
    tile_bf16 = pltpu.VMEM((ns, tt, w), BF16)
    tile_f32 = pltpu.VMEM((ns, tt, w), F32)
    scratch_set = [tile_bf16] * 7 + [tile_f32] * 2 + [pltpu.VMEM((ns, tt // RW_CHUNK, 1, w), F32)]
    p3 = p_rw.reshape(batch, seq, n)
    hsum = _head_sum_matrix(LANES, HEAD_DIM)
    trow = lax.broadcasted_iota(jnp.int32, (tt, tt), 0)
    tcol = lax.broadcasted_iota(jnp.int32, (tt, tt), 1)
    cum = jnp.logical_and(trow // RW_CHUNK == tcol // RW_CHUNK, trow >= tcol).astype(BF16)
    whole = lambda a: pl.BlockSpec(a.shape, lambda b, t: (0, 0))
    out = pl.pallas_call(
        _rwkv_kernel,
        grid=(batch // ns, nt),
        in_specs=[pl.BlockSpec((ns, tt, n), lambda b, t: (b, t, 0)),
                  pl.BlockSpec((ns, 8, n), lambda b, t: (b, jnp.maximum(t * (tt // 8) - 1, 0), 0)),
                  vec(n), vec(w), mat, vec(w), mat, mat, vec(w), vec(w), vec(w), vec(w), vec(w),
                  whole(hsum), whole(cum)],
        out_specs=pl.BlockSpec((ns, tt, w), lambda b, t: (b, t, 0)),
        out_shape=jax.ShapeDtypeStruct((batch, seq, w), BF16),
        scratch_shapes=[pltpu.VMEM((ns * (w // LANES), LANES, LANES), F32), tile_f32] + scratch_set,
        compiler_params=_params("parallel", "arbitrary"),
        name="rwkv7",
    )(p3, p3, mu.reshape(1, n), w0.reshape(1, w), w_up_pad, a0.reshape(1, w), a_up_pad, g_up,
      k_k.reshape(1, w), k_a.reshape(1, w), r_k.reshape(1, w), gn_g.reshape(1, w), gn_b.reshape(1, w),
      hsum, cum)
    return out.reshape(batch * seq, w)


def _hgrn_kernel(q_ref, f_ref, i_ref, g_ref, lbl_ref, ng_ref, o_ref, state_ref, *, layer):
    hc = HG_CHUNK
    tile = min(HG_TILE, q_ref.shape[0])
    nch = tile // hc
    seq = q_ref.shape[0]
    logits = lbl_ref[...]
    e = jnp.exp(logits - jnp.max(logits, axis=0, keepdims=True))
    probs = e / jnp.sum(e, axis=0, keepdims=True)
    lb = jnp.zeros((1, LANES), F32)
    for j in range(1, layer + 1):
        lb = lb + probs[j:j + 1, :]
    ng = ng_ref[...]
    pos = _mod_pow2(lax.broadcasted_iota(jnp.int32, (tile, 1), 0), hc)
    assert hc == 2 * SUBLANES
    pos8 = lax.broadcasted_iota(jnp.int32, (nch, SUBLANES, 1), 1)
    half_rows = nch * SUBLANES
    same_chunk = (_div_pow2(lax.broadcasted_iota(jnp.int32, (half_rows, half_rows), 0), SUBLANES)
                  == _div_pow2(lax.broadcasted_iota(jnp.int32, (half_rows, half_rows), 1), SUBLANES))
    state_ref[...] = jnp.zeros_like(state_ref)

    def tile_body(ti, _):
        rows = pl.ds(pl.multiple_of(ti * tile, tile), tile)
        q = q_ref[rows, :]
        fgate = lb + (1.0 - lb) * _sigmoid(f_ref[rows, :])
        xi = i_ref[rows, :]
        v = xi * _sigmoid(xi)
        k = 1.0 - fgate
        b = jnp.log(fgate)
        for d in (1, 2, 4, 8):
            b = b + jnp.where(pos >= d, pltpu.roll(b, d, 0), 0.0)
        b3 = b.reshape(nch, hc, LANES)
        q3 = q.reshape(nch, hc, LANES)
        k3 = k.reshape(nch, hc, LANES)
        v3 = v.reshape(nch, hc, LANES)
        tgt = b3 * LOG2E
        src = (b3 - jnp.log(k3)) * LOG2E
        lo, hi = slice(0, SUBLANES), slice(SUBLANES, hc)
        o_half = [jnp.zeros((nch, SUBLANES, LANES), F32) for _ in range(2)]
        for s in range(hc):
            h = s // SUBLANES
            rows_h = (lo, hi)[h]
            arg = tgt[:, rows_h, :] - src[:, s:s + 1, :]
            if s % SUBLANES:
                arg = jnp.where(pos8 >= s % SUBLANES, arg, -1e30)
            wgt = jnp.sum(q3[:, rows_h, :] * jnp.exp2(arg), axis=-1, keepdims=True)
            o_half[h] = o_half[h] + wgt * v3[:, s:s + 1, :]
        b_last = b3[:, hc - 1:hc, :]
        qd = (q3 * jnp.exp(b3)).astype(BF16)
        k_end = (k3 * jnp.exp(b_last - b3)).astype(BF16)
        carry = jnp.exp(b_last)
        vb = v3.astype(BF16)
        updates = [_dot_tn(vb[ch], k_end[ch]) for ch in range(nch)]
        states = [state_ref[...]]
        for ch in range(nch):
            states.append(states[ch] * carry[ch] + updates[ch])
        state_ref[...] = states[nch]
        inter = [_dot_nt(qd[ch], states[ch].astype(BF16)) for ch in range(nch)]
        ref = tgt[:, SUBLANES - 1:SUBLANES, :]
        q_hi = (q3[:, hi, :] * jnp.exp2(tgt[:, hi, :] - ref)).reshape(nch * SUBLANES, LANES).astype(BF16)
        k_lo = jnp.exp2(ref - src[:, lo, :]).reshape(nch * SUBLANES, LANES).astype(BF16)
        cross = jnp.where(same_chunk, _dot_nt(q_hi, k_lo), 0.0).astype(BF16)
        v_lo = v3[:, lo, :].reshape(nch * SUBLANES, LANES).astype(BF16)
        o_cross = _dot(cross, v_lo).reshape(nch, SUBLANES, LANES)
        o3 = jnp.concatenate([o_half[0], o_half[1] + o_cross], axis=1)
        o = o3.reshape(tile, LANES) + jnp.concatenate(inter, axis=0)
        o = o * lax.rsqrt(jnp.mean(o * o, axis=-1, keepdims=True) + RMS_EPS)
        gt = g_ref[rows, :]
        o_ref[rows, :] = (o * ng * (gt * _sigmoid(gt))).astype(o_ref.dtype)
        return 0

    lax.fori_loop(0, seq // tile, tile_body, 0)


def _hgrn(p_hg, batch, seq, lb_logits, norm_g, layer):
    nh = HGRN_HEADS
    x3 = p_hg.reshape(batch, seq, 4 * MIX_W)
    col = lambda j: pl.BlockSpec((None, seq, LANES), lambda b, h: (b, 0, j * nh + h))
    out = pl.pallas_call(
        functools.partial(_hgrn_kernel, layer=layer),
        grid=(batch, nh),
        in_specs=[col(0), col(1), col(2), col(3),
                  pl.BlockSpec((DEPTH, LANES), lambda b, h: (0, h)),
                  pl.BlockSpec((1, LANES), lambda b, h: (0, h))],
        out_specs=pl.BlockSpec((None, seq, LANES), lambda b, h: (b, 0, h)),
        out_shape=jax.ShapeDtypeStruct((batch, seq, MIX_W), BF16),
        scratch_shapes=[pltpu.VMEM((LANES, LANES), F32)],
        compiler_params=_params("parallel", "parallel"),
        name="hgrn2",
    )(x3, x3, x3, x3, lb_logits, norm_g.reshape(1, MIX_W))
    return out.reshape(batch * seq, MIX_W)


def _merge_kernel(h_ref, ysb_ref, yrw_ref, yhg_ref, gate_ref, wup_ref, wout_ref, g_ref, b_ref, o_ref):
    d = D_MODEL
    tm = h_ref.shape[0]
    sub = min(tm, DENSE_SUBTILE)
    for r0 in range(0, tm, sub):
        rows = slice(r0, r0 + sub)
        merged = None
        for j, y_ref in enumerate((ysb_ref, yrw_ref, yhg_ref)):
            term = gate_ref[rows, j * d:(j + 1) * d].astype(F32) * _dot(y_ref[rows, :], wup_ref[j])
            merged = term if merged is None else merged + term
        x = ALPHA * h_ref[rows, :] + _dot(merged.astype(BF16), wout_ref[...])
        o_ref[rows, :] = _layer_norm(x, g_ref[...], b_ref[...])


def _merge(h, y_sb, y_rw, y_hg, gates, w_up, w_out, layer, g, b):
    m, d = h.shape
    tm = _row_tile(m, 512)
    rows = lambda width: pl.BlockSpec((tm, width), lambda i: (i, 0))
    return pl.pallas_call(
        _merge_kernel,
        grid=(m // tm,),
        in_specs=[rows(d), rows(MIX_W), rows(MIX_W), rows(MIX_W), rows(N_BRANCH * d),
                  pl.BlockSpec((None, N_BRANCH, MIX_W, d), lambda i: (layer, 0, 0, 0)),
                  pl.BlockSpec((None, d, d), lambda i: (layer, 0, 0)),
                  pl.BlockSpec((1, d), lambda i: (0, 0)),
                  pl.BlockSpec((1, d), lambda i: (0, 0))],
        out_specs=rows(d),
        out_shape=jax.ShapeDtypeStruct((m, d), F32),
        compiler_params=_params("parallel"),
        name="merge_out",
    )(h, y_sb, y_rw, y_hg, gates, w_up, w_out, g.reshape(1, d), b.reshape(1, d))


def _ffn_kernel(h_ref, wa_ref, wb_ref, wo_ref, g_ref, b_ref, o_ref):
    tm = h_ref.shape[0]
    sub = min(tm, DENSE_SUBTILE)
    for r0 in range(0, tm, sub):
        h = h_ref[r0:r0 + sub, :]
        hb = h.astype(BF16)
        a = _dot(hb, wa_ref[...])
        act = (a * _sigmoid(a) * _dot(hb, wb_ref[...])).astype(BF16)
        o_ref[r0:r0 + sub, :] = _layer_norm(ALPHA * h + _dot(act, wo_ref[...]), g_ref[...], b_ref[...])


def _ffn(h, w_in, w_out, layer, g, b):
    m, d = h.shape
    tm = _row_tile(m, 512)
    return pl.pallas_call(
        _ffn_kernel,
        grid=(m // tm,),
        in_specs=[pl.BlockSpec((tm, d), lambda i: (i, 0)),
                  pl.BlockSpec((None, d, D_FF), lambda i: (layer, 0, 0)),
                  pl.BlockSpec((None, d, D_FF), lambda i: (layer, 0, 1)),
                  pl.BlockSpec((None, D_FF, d), lambda i: (layer, 0, 0)),
                  pl.BlockSpec((1, d), lambda i: (0, 0)),
                  pl.BlockSpec((1, d), lambda i: (0, 0))],
        out_specs=pl.BlockSpec((tm, d), lambda i: (i, 0)),
        out_shape=jax.ShapeDtypeStruct((m, d), F32),
        compiler_params=_params("parallel"),
        name="ffn",
    )(h, w_in, w_in, w_out, g.reshape(1, d), b.reshape(1, d))


def kernel(x, ln_in_g, ln_in_b, w_in, rwkv_mu, rwkv_w0, rwkv_w_up, rwkv_a0, rwkv_a_up, rwkv_g_up,
           rwkv_k_k, rwkv_k_a, rwkv_r_k, rwkv_ln_g, rwkv_ln_b, hgrn_lb_logits, hgrn_norm_g,
           w_branch_up, w_out, ln1_g, ln1_b, w_ffn_in, w_ffn_out, ln2_g, ln2_b):
    batch, seq, d = x.shape
    m = batch * seq
    w = MIX_W
    assert w_in.shape[-1] == sum(PROJ_WIDTHS)
    w_in_b = w_in.astype(BF16)
    zeros_lora = jnp.zeros((DEPTH, DECAY_LORA, w), BF16)
    w_up_pad = jnp.concatenate([rwkv_w_up.astype(BF16), zeros_lora], axis=1)
    a_up_pad = jnp.concatenate([zeros_lora, rwkv_a_up.astype(BF16)], axis=1)
    g_up_b = rwkv_g_up.astype(BF16)
    w_branch_b = w_branch_up.astype(BF16)
    w_out_b = w_out.astype(BF16)
    w_ffn_in_b = w_ffn_in.astype(BF16)
    w_ffn_out_b = w_ffn_out.astype(BF16)
    lb_logits = hgrn_lb_logits.astype(F32)

    h = _entry_ln(x.reshape(m, d), ln_in_g, ln_in_b)
    for l in range(DEPTH):
        qkv, p_rw, p_hg, gates = _project(h, w_in_b, l)
        y_sb = _sb_attention(qkv, batch, seq)
        y_rw = _rwkv(p_rw, batch, seq, rwkv_mu[l], rwkv_w0[l], w_up_pad[l], rwkv_a0[l], a_up_pad[l],
                     g_up_b[l], rwkv_k_k[l], rwkv_k_a[l], rwkv_r_k[l], rwkv_ln_g[l], rwkv_ln_b[l])
        y_hg = _hgrn(p_hg, batch, seq, lb_logits, hgrn_norm_g[l], l)
        h = _merge(h, y_sb, y_rw, y_hg, gates, w_branch_b, w_out_b, l, ln1_g[l], ln1_b[l])
        h = _ffn(h, w_ffn_in_b, w_ffn_out_b, l, ln2_g[l], ln2_b[l])
    return h.reshape(batch, seq, d).astype(x.dtype)
```
